```python
import math
import jax, jax.numpy as jnp
from jax import lax
import numpy as np

D_MODEL = 1024
BATCH = 4
SEQ = 8192
DEPTH = 4

D_MIX = D_MODEL
CONV_DIM = D_MIX // 2
CONV_GROUPS = 8
CONV_WIDTH = 31
GLA_HEADS = 4
GLA_VAL_DIM = D_MIX // 2
GLA_HEAD_V = GLA_VAL_DIM // GLA_HEADS
GLA_KEY_DIM = GLA_VAL_DIM // 2
GLA_HEAD_K = GLA_KEY_DIM // GLA_HEADS
GATE_RANK = 16
GATE_TAU = 16.0
GLA_CHUNK = 64
IN_COLS = 2 * CONV_DIM + 2 * GLA_KEY_DIM + 2 * GLA_VAL_DIM + GATE_RANK
MEM_LEN = 256
X_HEADS = 4
X_HEAD_DIM = D_MODEL // X_HEADS
D_FF = int(math.ceil(8 * D_MODEL / 3 / 256) * 256)
DEEPNORM_ALPHA = (2 * DEPTH) ** 0.25
DEEPNORM_BETA = (8 * DEPTH) ** -0.25
LN_EPS = 1e-5

kernel_name = "hybrid_conv_gla_deepnorm_trunk"


def layer_norm(x, g, b):
    xf = x.astype(jnp.float32)
    mu = jnp.mean(xf, axis=-1, keepdims=True)
    xc = xf - mu
    var = jnp.mean(xc * xc, axis=-1, keepdims=True)
    return (xc * lax.rsqrt(var + LN_EPS) * g.astype(jnp.float32) + b.astype(jnp.float32)).astype(x.dtype)


def rms_norm(x, g):
    xf = x.astype(jnp.float32)
    ms = jnp.mean(xf * xf, axis=-1, keepdims=True)
    return (xf * lax.rsqrt(ms + LN_EPS) * g.astype(jnp.float32)).astype(x.dtype)


def causal_depthwise_conv(u, w, b):
    c = u.shape[-1]
    y = lax.conv_general_dilated(
        u, w[:, None, :].astype(u.dtype), window_strides=(1,),
        padding=[(w.shape[0] - 1, 0)],
        dimension_numbers=("NWC", "WIO", "NWC"),
        feature_group_count=c)
    return y + b


def gla_chunked(q, k, v, log_a):
    out_dtype = v.dtype
    bsz, t, h, dk = q.shape
    dv = v.shape[-1]
    n = t // GLA_CHUNK

    def to_chunks(z):
        return z.astype(jnp.float32).reshape(bsz, n, GLA_CHUNK, h, z.shape[-1]).transpose(0, 3, 1, 2, 4)

    qc = to_chunks(q) * (dk ** -0.5)
    kc, vc, gc = to_chunks(k), to_chunks(v), to_chunks(log_a)
    bcum = jnp.cumsum(gc, axis=3)
    b_last = bcum[:, :, :, -1:, :]
    qe = qc * jnp.exp(bcum)
    ke = kc * jnp.exp(-bcum)
    kd = kc * jnp.exp(b_last - bcum)
    causal = jnp.tril(jnp.ones((GLA_CHUNK, GLA_CHUNK), dtype=bool))
    att = jnp.einsum("bhncd,bhnsd->bhncs", qe, ke)
    att = jnp.where(causal, att, 0.0)
    o_intra = jnp.einsum("bhncs,bhnse->bhnce", att, vc)
    upd = jnp.einsum("bhncd,bhnce->bhnde", kd, vc)
    decay = jnp.exp(b_last[:, :, :, 0, :])

    def step(state, inp):
        dec, u = inp
        return dec[..., None] * state + u, state

    s0 = jnp.zeros((bsz, h, dk, dv), jnp.float32)
    _, s_prev = lax.scan(step, s0, (jnp.moveaxis(decay, 2, 0), jnp.moveaxis(upd, 2, 0)))
    s_prev = jnp.moveaxis(s_prev, 0, 2)
    o_inter = jnp.einsum("bhncd,bhnde->bhnce", qe, s_prev)
    o = (o_intra + o_inter).transpose(0, 2, 3, 1, 4).reshape(bsz, t, h, dv)
    return o.astype(out_dtype)


def hybrid_mixer(h, w_in, w_a2, b_a, conv_w, conv_b, conv_ln_g, conv_ln_b, gla_norm_g, w_out):
    bsz, t, _ = h.shape
    proj = h @ w_in
    cuts = np.cumsum([CONV_DIM, CONV_DIM, GLA_KEY_DIM, GLA_KEY_DIM, GLA_VAL_DIM, GLA_VAL_DIM]).tolist()
    c_a, c_g, q, k, v, r, a_low = jnp.split(proj, cuts, axis=-1)
    u = c_a * jax.nn.sigmoid(c_g)
    u = causal_depthwise_conv(u, conv_w, conv_b)
    u = jax.nn.silu(layer_norm(u, conv_ln_g, conv_ln_b))
    z = a_low @ w_a2 + b_a
    log_a = jax.nn.log_sigmoid(z.astype(jnp.float32)) / GATE_TAU
    heads = lambda y, d: y.reshape(bsz, t, GLA_HEADS, d)
    o = gla_chunked(heads(q, GLA_HEAD_K), heads(k, GLA_HEAD_K), heads(v, GLA_HEAD_V),
                    heads(log_a, GLA_HEAD_K))
    o = rms_norm(o, gla_norm_g) * jax.nn.silu(heads(r, GLA_HEAD_V))
    o = o.reshape(bsz, t, GLA_VAL_DIM)
    return jnp.concatenate([u, o], axis=-1) @ w_out


def memory_cross_attention(x, mem, w_q, w_kv, w_o):
    bsz, t, _ = x.shape
    q = (x @ w_q).reshape(bsz, t, X_HEADS, X_HEAD_DIM)
    kv = mem @ w_kv
    k, v = jnp.split(kv, 2, axis=-1)
    k = k.reshape(bsz, MEM_LEN, X_HEADS, X_HEAD_DIM)
    v = v.reshape(bsz, MEM_LEN, X_HEADS, X_HEAD_DIM)
    s = jnp.einsum("bthd,bmhd->bhtm", q.astype(jnp.float32), k.astype(jnp.float32)) * (X_HEAD_DIM ** -0.5)
    p = jax.nn.softmax(s, axis=-1).astype(v.dtype)
    o = jnp.einsum("bhtm,bmhd->bthd", p, v).reshape(bsz, t, D_MODEL)
    return o @ w_o


def swiglu(x, w_in, w_out):
    g, u = jnp.split(x @ w_in, 2, axis=-1)
    return (jax.nn.silu(g) * u) @ w_out


def setup_inputs(seed: int = 0) -> dict:
    key = jax.random.key(seed)
    ks = jax.random.split(key, 24)
    nrm = lambda k, shape, scale: jax.random.normal(k, shape, jnp.float32) * scale
    gain = lambda k, shape: 1.0 + nrm(k, shape, 0.02)
    L = DEPTH
    return {
        "x": nrm(ks[0], (BATCH, SEQ, D_MODEL), 1.0),
        "mem": nrm(ks[1], (BATCH, MEM_LEN, D_MODEL), 1.0),
        "ln0_g": gain(ks[2], (D_MODEL,)),
        "ln0_b": nrm(ks[3], (D_MODEL,), 0.02),
        "w_in": nrm(ks[4], (L, D_MODEL, IN_COLS), D_MODEL ** -0.5),
        "w_a2": nrm(ks[5], (L, GATE_RANK, GLA_KEY_DIM), GATE_RANK ** -0.5),
        "b_a": nrm(ks[6], (L, GLA_KEY_DIM), 0.01),
        "conv_w": nrm(ks[7], (L, CONV_WIDTH, CONV_DIM), CONV_WIDTH ** -0.5),
        "conv_b": nrm(ks[8], (L, CONV_DIM), 0.02),
        "conv_ln_g": gain(ks[9], (L, CONV_DIM)),
        "conv_ln_b": nrm(ks[10], (L, CONV_DIM), 0.02),
        "gla_norm_g": gain(ks[11], (L, GLA_HEAD_V)),
        "w_mix_out": nrm(ks[12], (L, D_MIX, D_MODEL), DEEPNORM_BETA * D_MIX ** -0.5),
        "ln1_g": gain(ks[13], (L, D_MODEL)),
        "ln1_b": nrm(ks[14], (L, D_MODEL), 0.02),
        "w_xq": nrm(ks[15], (L, D_MODEL, D_MODEL), D_MODEL ** -0.5),
        "w_xkv": jnp.concatenate([
            nrm(ks[16], (L, D_MODEL, D_MODEL), D_MODEL ** -0.5),
            nrm(jax.random.fold_in(ks[16], 1), (L, D_MODEL, D_MODEL), DEEPNORM_BETA * D_MODEL ** -0.5)], axis=-1),
        "w_xo": nrm(ks[17], (L, D_MODEL, D_MODEL), DEEPNORM_BETA * D_MODEL ** -0.5),
        "ln2_g": gain(ks[18], (L, D_MODEL)),
        "ln2_b": nrm(ks[19], (L, D_MODEL), 0.02),
        "w_ffn_in": nrm(ks[20], (L, D_MODEL, 2 * D_FF), D_MODEL ** -0.5),
        "w_ffn_out": nrm(ks[21], (L, D_FF, D_MODEL), DEEPNORM_BETA * D_FF ** -0.5),
        "ln3_g": gain(ks[22], (L, D_MODEL)),
        "ln3_b": nrm(ks[23], (L, D_MODEL), 0.02),
    }


def reference(x, mem, ln0_g, ln0_b, w_in, w_a2, b_a, conv_w, conv_b, conv_ln_g, conv_ln_b,
              gla_norm_g, w_mix_out, ln1_g, ln1_b, w_xq, w_xkv, w_xo, ln2_g, ln2_b,
              w_ffn_in, w_ffn_out, ln3_g, ln3_b):
    h = layer_norm(x, ln0_g, ln0_b)
    for l in range(DEPTH):
        mix = hybrid_mixer(h, w_in[l], w_a2[l], b_a[l], conv_w[l], conv_b[l], conv_ln_g[l],
                           conv_ln_b[l], gla_norm_g[l], w_mix_out[l])
        h = layer_norm(DEEPNORM_ALPHA * h + mix, ln1_g[l], ln1_b[l])
        xa = memory_cross_attention(h, mem, w_xq[l], w_xkv[l], w_xo[l])
        h = layer_norm(DEEPNORM_ALPHA * h + xa, ln2_g[l], ln2_b[l])
        ff = swiglu(h, w_ffn_in[l], w_ffn_out[l])
        h = layer_norm(DEEPNORM_ALPHA * h + ff, ln3_g[l], ln3_b[l])
    return h
```

```python
import functools
import math

import jax
import jax.numpy as jnp
from jax import lax
from jax.experimental import pallas as pl
from jax.experimental.pallas import tpu as pltpu

F32 = jnp.float32
BF16 = jnp.bfloat16

LN_EPS = 1e-5
GATE_TAU = 16.0
GLA_CHUNK = 64
GLA_HEADS = 4
X_HEADS = 4

LANES = 128
SUBLANES = 8
V7X_VMEM_LIMIT_BYTES = 60000 * 1024

MIX_ROWS = 512
ATT_ROWS = 512
FFN_ROWS = 512
LN_ROWS = 1024
CONV_ROWS = 32
HALO = 32

_NT = (((1,), (1,)), ((), ()))
_TN = (((0,), (0,)), ((), ()))


def _dot(a, b):
    return jnp.dot(a, b, preferred_element_type=F32)


def _dot_nt(a, b):
    return lax.dot_general(a, b, _NT, preferred_element_type=F32)


def _dot_tn(a, b):
    return lax.dot_general(a, b, _TN, preferred_element_type=F32)


def _layer_norm(x, g, b):
    mu = jnp.mean(x, axis=-1, keepdims=True)
    xc = x - mu
    var = jnp.mean(xc * xc, axis=-1, keepdims=True)
    return xc * lax.rsqrt(var + LN_EPS) * g + b


def _sigmoid(x):
    return 1.0 / (1.0 + jnp.exp(-x))


def _silu(x):
    return x * _sigmoid(x)


def _log_sigmoid(x):
    return jnp.minimum(x, 0.0) - jnp.log(1.0 + jnp.exp(-jnp.abs(x)))


def _ln_kernel(x_ref, g_ref, b_ref, o_ref):
    o_ref[...] = _layer_norm(x_ref[...], g_ref[...], b_ref[...])


def _entry_layer_norm(x2, g, b):
    n, d = x2.shape
    return pl.pallas_call(
        _ln_kernel,
        out_shape=jax.ShapeDtypeStruct((n, d), F32),
        grid=(n // LN_ROWS,),
        in_specs=[pl.BlockSpec((LN_ROWS, d), lambda i: (i, 0)),
                  pl.BlockSpec((1, d), lambda i: (0, 0)),
                  pl.BlockSpec((1, d), lambda i: (0, 0))],
        out_specs=pl.BlockSpec((LN_ROWS, d), lambda i: (i, 0)),
        compiler_params=pltpu.CompilerParams(dimension_semantics=("parallel",)),
        name="entry_ln",
    )(x2, g.reshape(1, d), b.reshape(1, d))


def _kv_kernel(mem_ref, w_ref, o_ref):
    o_ref[...] = _dot(mem_ref[...].astype(BF16), w_ref[...]).astype(BF16)


def _memory_kv(mem, w_xkv16):
    bsz, m, d = mem.shape
    depth, _, n = w_xkv16.shape
    return pl.pallas_call(
        _kv_kernel,
        out_shape=jax.ShapeDtypeStruct((depth, bsz, m, n), BF16),
        grid=(depth, bsz),
        in_specs=[pl.BlockSpec((None, m, d), lambda l, b: (b, 0, 0)),
                  pl.BlockSpec((None, d, n), lambda l, b: (l, 0, 0))],
        out_specs=pl.BlockSpec((None, None, m, n), lambda l, b: (l, b, 0, 0)),
        compiler_params=pltpu.CompilerParams(dimension_semantics=("arbitrary", "arbitrary")),
        name="memory_kv",
    )(mem, w_xkv16)


def _mixer_kernel(h_ref, w_in_ref, w_al_ref, w_a2_ref, b_a_ref, cw_ref, cb_ref, cg_ref, cbeta_ref,
                  gn_ref, w_out_ref, lg_ref, lb_ref, o_ref,
                  u_s, q_s, k_s, g_s, v_s, og_s, xo_s, st_s, *, alpha, taps, conv_dim, key_dim,
                  val_dim):
    rows = h_ref.shape[0]
    c = conv_dim
    dk = key_dim
    dv = val_dim
    head_v = dv // GLA_HEADS
    head_k = dk // GLA_HEADS
    n_u = rows + HALO

    @pl.when(pl.program_id(1) == 0)
    def _():
        u_s[0, 0:HALO, :] = jnp.zeros((HALO, c), F32)
        st_s[...] = jnp.zeros(st_s.shape, F32)

    h16 = h_ref[...].astype(BF16)

    ca = _dot(h16, w_in_ref[:, 0:c])
    cgate = _dot(h16, w_in_ref[:, c:2 * c])
    u_s[0, HALO:n_u, :] = ca * _sigmoid(cgate)
    u_all = u_s[0]
    for s in range(1, SUBLANES):
        u_s[s] = pltpu.roll(u_all, n_u - s, 0)

    first = HALO - (taps - 1)

    def conv_block(rb, carry):
        r0 = pl.multiple_of(rb * CONV_ROWS, CONV_ROWS)
        acc = jnp.zeros((CONV_ROWS, c), F32)
        for k in range(taps):
            a, s = divmod(first + k, SUBLANES)
            xs = u_s[s, pl.ds(pl.multiple_of(r0 + SUBLANES * a, SUBLANES), CONV_ROWS), :]
            acc = acc + cw_ref[k:k + 1, :] * xs
        y = _layer_norm(acc + cb_ref[...], cg_ref[...], cbeta_ref[...])
        xo_s[pl.ds(r0, CONV_ROWS), 0:c] = _silu(y).astype(BF16)
        return carry

    lax.fori_loop(0, rows // CONV_ROWS, conv_block, 0)
    u_s[0, 0:HALO, :] = u_s[0, rows:n_u, :]

    o_q = 2 * c
    q_s[...] = _dot(h16, w_in_ref[:, o_q:o_q + dk]) * (head_k ** -0.5)
    k_s[...] = _dot(h16, w_in_ref[:, o_q + dk:o_q + 2 * dk])
    o_v = o_q + 2 * dk
    v_s[...] = _dot(h16, w_in_ref[:, o_v:o_v + dv]).astype(BF16)
    a_low = _dot(h16, w_al_ref[...])
    z = _dot(a_low.astype(BF16), w_a2_ref[...]) + b_a_ref[...]
    g_s[...] = _log_sigmoid(z) * (1.0 / GATE_TAU)

    ch = GLA_CHUNK
    row_i = lax.broadcasted_iota(jnp.int32, (ch, ch), 0)
    col_i = lax.broadcasted_iota(jnp.int32, (ch, ch), 1)
    causal = row_i >= col_i
    tri16 = causal.astype(BF16)
    lane_i = lax.broadcasted_iota(jnp.int32, (ch, 2 * head_k), 1)

    def gla_chunk(ci, carry):
        r0 = pl.multiple_of(ci * ch, ch)
        g = g_s[pl.ds(r0, ch), :]
        g_hi = g.astype(BF16)
        g_lo = (g - g_hi.astype(F32)).astype(BF16)
        bcum = _dot(tri16, g_hi) + _dot(tri16, g_lo)
        b_last = bcum[ch - 1:ch, :]
        q = q_s[pl.ds(r0, ch), :]
        k = k_s[pl.ds(r0, ch), :]
        qe = q * jnp.exp(bcum)
        ke = (k * jnp.exp(-bcum)).astype(BF16)
        kd = (k * jnp.exp(b_last - bcum)).astype(BF16)
        decay = jnp.exp(b_last)
        for j in range(GLA_HEADS // 2):
            sl = slice(2 * head_k * j, 2 * head_k * (j + 1))
            qe_j = qe[:, sl]
            ke_j = ke[:, sl]
            kd_j = kd[:, sl]
            decay_j = decay[:, sl]
            for hh in range(2):
                hd = 2 * j + hh
                own = (lane_i >= hh * head_k) & (lane_i < (hh + 1) * head_k)
                qm = jnp.where(own, qe_j, 0.0).astype(BF16)
                att = jnp.where(causal, _dot_nt(qm, ke_j), 0.0)
                v_h = v_s[pl.ds(r0, ch), hd * head_v:(hd + 1) * head_v]
                st = st_s[hd]
                o = _dot(att.astype(BF16), v_h) + _dot_nt(qm, st.astype(BF16))
                og_s[pl.ds(r0, ch), hd * head_v:(hd + 1) * head_v] = o
                st_s[hd] = st * decay_j + _dot_tn(v_h, kd_j)
        return carry

    lax.fori_loop(0, rows // ch, gla_chunk, 0)

    o_r = o_v + dv
    for hd in range(GLA_HEADS):
        sl = slice(hd * head_v, (hd + 1) * head_v)
        o = og_s[:, sl]
        ms = jnp.mean(o * o, axis=-1, keepdims=True)
        r = _dot(h16, w_in_ref[:, o_r + hd * head_v:o_r + (hd + 1) * head_v])
        xo_s[:, c + hd * head_v:c + (hd + 1) * head_v] = (
            o * lax.rsqrt(ms + LN_EPS) * gn_ref[...] * _silu(r)).astype(BF16)

    mix = _dot(xo_s[...], w_out_ref[...])
    o_ref[...] = _layer_norm(alpha * h_ref[...] + mix, lg_ref[...], lb_ref[...])


def _const_spec(shape, layer):
    nd = len(shape)
    return pl.BlockSpec((None,) + tuple(shape), lambda *_: (layer,) + (0,) * nd)


def _mixer(h, layer, p, alpha):
    bsz, t, d = h.shape
    rows = MIX_ROWS
    taps, c = p["conv_w"].shape[1:]
    dk = p["w_a2"].shape[2]
    dv = p["w_main"].shape[2] - 2 * c - 2 * dk
    dv //= 2
    kern = functools.partial(_mixer_kernel, alpha=alpha, taps=taps, conv_dim=c, key_dim=dk,
                             val_dim=dv)
    names = ["w_main", "w_al", "w_a2", "b_a", "conv_w", "conv_b", "conv_ln_g", "conv_ln_b",
             "gla_norm_g", "w_mix_out", "ln1_g", "ln1_b"]
    return pl.pallas_call(
        kern,
        out_shape=jax.ShapeDtypeStruct((bsz, t, d), F32),
        grid=(bsz, t // rows),
        in_specs=[pl.BlockSpec((None, rows, d), lambda b, i: (b, i, 0))]
        + [_const_spec(p[n].shape[1:], layer) for n in names],
        out_specs=pl.BlockSpec((None, rows, d), lambda b, i: (b, i, 0)),
        scratch_shapes=[
            pltpu.VMEM((SUBLANES, rows + HALO, c), F32),
            pltpu.VMEM((rows, dk), F32),
            pltpu.VMEM((rows, dk), F32),
            pltpu.VMEM((rows, dk), F32),
            pltpu.VMEM((rows, dv), BF16),
            pltpu.VMEM((rows, dv), F32),
            pltpu.VMEM((rows, c + dv), BF16),
            pltpu.VMEM((GLA_HEADS, dv // GLA_HEADS, 2 * dk // GLA_HEADS), F32),
        ],
        compiler_params=pltpu.CompilerParams(
            dimension_semantics=("arbitrary", "arbitrary"),
            vmem_limit_bytes=V7X_VMEM_LIMIT_BYTES),
        name="mixer",
    )(h, *[p[n] for n in names])


def _xattn_kernel(h_ref, wq_ref, kv_ref, wo_ref, lg_ref, lb_ref, o_ref, *, alpha):
    d = h_ref.shape[1]
    hd = d // X_HEADS
    h = h_ref[...]
    q = (_dot(h.astype(BF16), wq_ref[...]) * (hd ** -0.5)).astype(BF16)
    outs = []
    for i in range(X_HEADS):
        sl = slice(i * hd, (i + 1) * hd)
        s = _dot_nt(q[:, sl], kv_ref[:, sl])
        e = jnp.exp(s - jnp.max(s, axis=-1, keepdims=True))
        p = e * (1.0 / jnp.sum(e, axis=-1, keepdims=True))
        outs.append(_dot(p.astype(BF16), kv_ref[:, d + i * hd:d + (i + 1) * hd]).astype(BF16))
    xa = _dot(jnp.concatenate(outs, axis=-1), wo_ref[...])
    o_ref[...] = _layer_norm(alpha * h + xa, lg_ref[...], lb_ref[...])


def _xattn(h, kv, layer, p, alpha):
    bsz, t, d = h.shape
    rows = ATT_ROWS
    m = kv.shape[2]
    names = ["w_xq", "w_xo", "ln2_g", "ln2_b"]
    specs = {n: _const_spec(p[n].shape[1:], layer) for n in names}
    return pl.pallas_call(
        functools.partial(_xattn_kernel, alpha=alpha),
        out_shape=jax.ShapeDtypeStruct((bsz, t, d), F32),
        grid=(bsz, t // rows),
        in_specs=[pl.BlockSpec((None, rows, d), lambda b, i: (b, i, 0)),
                  specs["w_xq"],
                  pl.BlockSpec((None, None, m, 2 * d), lambda b, i: (layer, b, 0, 0)),
                  specs["w_xo"], specs["ln2_g"], specs["ln2_b"]],
        out_specs=pl.BlockSpec((None, rows, d), lambda b, i: (b, i, 0)),
        compiler_params=pltpu.CompilerParams(
            dimension_semantics=("parallel", "parallel"),
            vmem_limit_bytes=V7X_VMEM_LIMIT_BYTES),
        name="xattn",
    )(h, p["w_xq"], kv, p["w_xo"], p["ln2_g"], p["ln2_b"])


def _ffn_kernel(h_ref, w1_ref, w2_ref, lg_ref, lb_ref, o_ref, *, alpha, chunks):
    h = h_ref[...]
    h16 = h.astype(BF16)
    dff = w2_ref.shape[0]
    ff = None
    start = 0
    for width in chunks:
        g = _dot(h16, w1_ref[:, start:start + width])
        u = _dot(h16, w1_ref[:, dff + start:dff + start + width])
        part = _dot((_silu(g) * u).astype(BF16), w2_ref[start:start + width, :])
        ff = part if ff is None else ff + part
        start += width
    o_ref[...] = _layer_norm(alpha * h + ff, lg_ref[...], lb_ref[...])


def _ffn_chunks(dff, mxu_cols=2 * LANES, target=3):
    tiles = dff // mxu_cols
    assert tiles * mxu_cols == dff
    out = []
    while tiles > 0:
        take = min(target, tiles)
        out.append(take * mxu_cols)
        tiles -= take
    return tuple(out)


def _ffn(h, layer, p, alpha):
    bsz, t, d = h.shape
    rows = FFN_ROWS
    dff = p["w_ffn_out"].shape[1]
    names = ["w_ffn_in", "w_ffn_out", "ln3_g", "ln3_b"]
    return pl.pallas_call(
        functools.partial(_ffn_kernel, alpha=alpha, chunks=_ffn_chunks(dff)),
        out_shape=jax.ShapeDtypeStruct((bsz, t, d), F32),
        grid=(bsz, t // rows),
        in_specs=[pl.BlockSpec((None, rows, d), lambda b, i: (b, i, 0))]
        + [_const_spec(p[n].shape[1:], layer) for n in names],
        out_specs=pl.BlockSpec((None, rows, d), lambda b, i: (b, i, 0)),
        compiler_params=pltpu.CompilerParams(
            dimension_semantics=("parallel", "parallel"),
            vmem_limit_bytes=V7X_VMEM_LIMIT_BYTES),
        name="ffn",
    )(h, *[p[n] for n in names])


def kernel(x, mem, ln0_g, ln0_b, w_in, w_a2, b_a, conv_w, conv_b, conv_ln_g, conv_ln_b, gla_norm_g,
           w_mix_out, ln1_g, ln1_b, w_xq, w_xkv, w_xo, ln2_g, ln2_b, w_ffn_in, w_ffn_out, ln3_g,
           ln3_b):
    bsz, t, d = x.shape
    depth = w_in.shape[0]
    rank = w_a2.shape[1]
    alpha = (2 * depth) ** 0.25
    n_main = w_in.shape[2] - rank

    row = lambda a: a.reshape(depth, 1, a.shape[-1])
    p = {
        "w_main": w_in[:, :, :n_main].astype(BF16),
        "w_al": jnp.pad(w_in[:, :, n_main:], ((0, 0), (0, 0), (0, LANES - rank))).astype(BF16),
        "w_a2": jnp.pad(w_a2, ((0, 0), (0, LANES - rank), (0, 0))).astype(BF16),
        "w_mix_out": w_mix_out.astype(BF16),
        "w_xq": w_xq.astype(BF16),
        "w_xo": w_xo.astype(BF16),
        "w_ffn_in": w_ffn_in.astype(BF16),
        "w_ffn_out": w_ffn_out.astype(BF16),
        "conv_w": conv_w,
        "b_a": row(b_a), "conv_b": row(conv_b), "conv_ln_g": row(conv_ln_g),
        "conv_ln_b": row(conv_ln_b), "gla_norm_g": row(gla_norm_g),
        "ln1_g": row(ln1_g), "ln1_b": row(ln1_b), "ln2_g": row(ln2_g), "ln2_b": row(ln2_b),
        "ln3_g": row(ln3_g), "ln3_b": row(ln3_b),
    }
    kv = _memory_kv(mem, w_xkv.astype(BF16))
    h = _entry_layer_norm(x.reshape(bsz * t, d), ln0_g, ln0_b).reshape(bsz, t, d)
    for layer in range(depth):
        h = _mixer(h, layer, p, alpha)
        h = _xattn(h, kv, layer, p, alpha)
        h = _ffn(h, layer, p, alpha)
    return h
```

```python
import functools
import math

import jax
import jax.numpy as jnp
from jax import lax
from jax.experimental import pallas as pl
from jax.experimental.pallas import tpu as pltpu

F32 = jnp.float32
BF16 = jnp.bfloat16

LN_EPS = 1e-5
GATE_TAU = 16.0
GLA_CHUNK = 64
GLA_HEADS = 4
X_HEADS = 4

LANES = 128
SUBLANES = 8
V7X_VMEM_LIMIT_BYTES = 60000 * 1024

MIX_ROWS = 512
ATT_ROWS = 512
FFN_ROWS = 512
LN_ROWS = 1024
CONV_ROWS = 32
HALO = 32

_NT = (((1,), (1,)), ((), ()))
_TN = (((0,), (0,)), ((), ()))


def _dot(a, b):
    return jnp.dot(a, b, preferred_element_type=F32)


def _dot_nt(a, b):
    return lax.dot_general(a, b, _NT, preferred_element_type=F32)


def _dot_tn(a, b):
    return lax.dot_general(a, b, _TN, preferred_element_type=F32)


def _layer_norm(x, g, b):
    mu = jnp.mean(x, axis=-1, keepdims=True)
    xc = x - mu
    var = jnp.mean(xc * xc, axis=-1, keepdims=True)
    return xc * lax.rsqrt(var + LN_EPS) * g + b


def _sigmoid(x):
    return 0.5 * jnp.tanh(0.5 * x) + 0.5


def _silu(x):
    half = 0.5 * x
    return half * jnp.tanh(half) + half


def _log_sigmoid(x):
    return jnp.minimum(x, 0.0) - jnp.log(1.0 + jnp.exp(-jnp.abs(x)))


def _ln_kernel(x_ref, g_ref, b_ref, o_ref):
    o_ref[...] = _layer_norm(x_ref[...], g_ref[...], b_ref[...])


def _entry_layer_norm(x2, g, b):
    n, d = x2.shape
    return pl.pallas_call(
        _ln_kernel,
        out_shape=jax.ShapeDtypeStruct((n, d), F32),
        grid=(n // LN_ROWS,),
        in_specs=[pl.BlockSpec((LN_ROWS, d), lambda i: (i, 0)),
                  pl.BlockSpec((1, d), lambda i: (0, 0)),
                  pl.BlockSpec((1, d), lambda i: (0, 0))],
        out_specs=pl.BlockSpec((LN_ROWS, d), lambda i: (i, 0)),
        compiler_params=pltpu.CompilerParams(dimension_semantics=("parallel",)),
        name="entry_ln",
    )(x2, g.reshape(1, d), b.reshape(1, d))


def _kv_kernel(mem_ref, w_ref, o_ref):
    o_ref[...] = _dot(mem_ref[...].astype(BF16), w_ref[...]).astype(BF16)


def _memory_kv(mem, w_xkv16):
    bsz, m, d = mem.shape
    depth, _, n = w_xkv16.shape
    return pl.pallas_call(
        _kv_kernel,
        out_shape=jax.ShapeDtypeStruct((depth, bsz, m, n), BF16),
        grid=(depth, bsz),
        in_specs=[pl.BlockSpec((None, m, d), lambda l, b: (b, 0, 0)),
                  pl.BlockSpec((None, d, n), lambda l, b: (l, 0, 0))],
        out_specs=pl.BlockSpec((None, None, m, n), lambda l, b: (l, b, 0, 0)),
        compiler_params=pltpu.CompilerParams(dimension_semantics=("arbitrary", "arbitrary")),
        name="memory_kv",
    )(mem, w_xkv16)


def _mixer_kernel(h_ref, w_in_ref, w_al_ref, w_a2_ref, b_a_ref, cw_ref, cb_ref, cg_ref, cbeta_ref,
                  gn_ref, w_out_ref, lg_ref, lb_ref, o_ref, u_s, st_s, *, alpha, taps, conv_dim,
                  key_dim, val_dim):
    rows = h_ref.shape[0]
    c = conv_dim
    dk = key_dim
    dv = val_dim
    head_v = dv // GLA_HEADS
    head_k = dk // GLA_HEADS
    slab = 2 * head_k
    n_u = rows + HALO
    ch = GLA_CHUNK
    n_ch = rows // ch

    @pl.when(pl.program_id(1) == 0)
    def _():
        u_s[0, 0:HALO, :] = jnp.zeros((HALO, c), F32)
        st_s[...] = jnp.zeros(st_s.shape, F32)

    h16 = h_ref[...].astype(BF16)

    ca = _dot(h16, w_in_ref[:, 0:c])
    cgate = _dot(h16, w_in_ref[:, c:2 * c])
    u_s[0, HALO:n_u, :] = ca * _sigmoid(cgate)
    u_all = u_s[0]
    for s in range(1, SUBLANES):
        u_s[s] = pltpu.roll(u_all, n_u - s, 0)

    first = HALO - (taps - 1)
    blocks = []
    for rb in range(rows // CONV_ROWS):
        r0 = rb * CONV_ROWS
        acc = None
        for k in range(taps):
            a, s = divmod(first + k, SUBLANES)
            term = cw_ref[k:k + 1, :] * u_s[s, r0 + SUBLANES * a:r0 + SUBLANES * a + CONV_ROWS, :]
            acc = term if acc is None else acc + term
        blocks.append(acc)
    u_s[0, 0:HALO, :] = u_s[0, rows:n_u, :]
    y = jnp.concatenate(blocks, axis=0) + cb_ref[...]
    conv_out = _silu(_layer_norm(y, cg_ref[...], cbeta_ref[...])).astype(BF16)

    o_q = 2 * c
    o_v = o_q + 2 * dk
    o_r = o_v + dv
    q = _dot(h16, w_in_ref[:, o_q:o_q + dk]) * (head_k ** -0.5)
    k = _dot(h16, w_in_ref[:, o_q + dk:o_q + 2 * dk])
    v16 = _dot(h16, w_in_ref[:, o_v:o_v + dv]).astype(BF16)
    a_low = _dot(h16, w_al_ref[...])
    z = _dot(a_low.astype(BF16), w_a2_ref[...]) + b_a_ref[...]
    g = _log_sigmoid(z) * (1.0 / GATE_TAU)
    g_hi = g.astype(BF16)
    g_lo = (g - g_hi.astype(F32)).astype(BF16)

    row_i = lax.broadcasted_iota(jnp.int32, (ch, ch), 0)
    col_i = lax.broadcasted_iota(jnp.int32, (ch, ch), 1)
    causal = row_i >= col_i
    tri16 = causal.astype(BF16)
    lane_i = lax.broadcasted_iota(jnp.int32, (ch, slab), 1)
    own = [(lane_i >= hh * head_k) & (lane_i < (hh + 1) * head_k) for hh in range(2)]

    qm = {}
    o_intra = {}
    upd = {}
    decay = []
    for ci in range(n_ch):
        rs = slice(ci * ch, (ci + 1) * ch)
        bcum = _dot(tri16, g_hi[rs]) + _dot(tri16, g_lo[rs])
        b_last = bcum[ch - 1:ch, :]
        qe = q[rs] * jnp.exp(bcum)
        ke = (k[rs] * jnp.exp(-bcum)).astype(BF16)
        kd = (k[rs] * jnp.exp(b_last - bcum)).astype(BF16)
        decay.append(jnp.exp(b_last))
        for hd in range(GLA_HEADS):
            j, hh = divmod(hd, 2)
            sl = slice(slab * j, slab * (j + 1))
            qm[ci, hd] = jnp.where(own[hh], qe[:, sl], 0.0).astype(BF16)
            att = jnp.where(causal, _dot_nt(qm[ci, hd], ke[:, sl]), 0.0)
            v_h = v16[rs, hd * head_v:(hd + 1) * head_v]
            o_intra[ci, hd] = _dot(att.astype(BF16), v_h)
            upd[ci, hd] = _dot_tn(v_h, kd[:, sl])

    gla_cols = []
    for hd in range(GLA_HEADS):
        j = hd // 2
        sl = slice(slab * j, slab * (j + 1))
        st = st_s[hd]
        outs = []
        for ci in range(n_ch):
            outs.append(o_intra[ci, hd] + _dot_nt(qm[ci, hd], st.astype(BF16)))
            st = st * decay[ci][:, sl] + upd[ci, hd]
        st_s[hd] = st
        o = jnp.concatenate(outs, axis=0)
        ms = jnp.mean(o * o, axis=-1, keepdims=True)
        r = _dot(h16, w_in_ref[:, o_r + hd * head_v:o_r + (hd + 1) * head_v])
        gla_cols.append((o * lax.rsqrt(ms + LN_EPS) * gn_ref[...] * _silu(r)).astype(BF16))

    mix = _dot(jnp.concatenate([conv_out] + gla_cols, axis=-1), w_out_ref[...])
    o_ref[...] = _layer_norm(alpha * h_ref[...] + mix, lg_ref[...], lb_ref[...])


def _const_spec(shape, layer):
    nd = len(shape)
    return pl.BlockSpec((None,) + tuple(shape), lambda *_: (layer,) + (0,) * nd,
                        pipeline_mode=pl.Buffered(1))


def _mixer(h, layer, p, alpha):
    bsz, t, d = h.shape
    rows = MIX_ROWS
    taps, c = p["conv_w"].shape[1:]
    dk = p["w_a2"].shape[2]
    dv = p["w_main"].shape[2] - 2 * c - 2 * dk
    dv //= 2
    kern = functools.partial(_mixer_kernel, alpha=alpha, taps=taps, conv_dim=c, key_dim=dk,
                             val_dim=dv)
    names = ["w_main", "w_al", "w_a2", "b_a", "conv_w", "conv_b", "conv_ln_g", "conv_ln_b",
             "gla_norm_g", "w_mix_out", "ln1_g", "ln1_b"]
    return pl.pallas_call(
        kern,
        out_shape=jax.ShapeDtypeStruct((bsz, t, d), F32),
        grid=(bsz, t // rows),
        in_specs=[pl.BlockSpec((None, rows, d), lambda b, i: (b, i, 0))]
        + [_const_spec(p[n].shape[1:], layer) for n in names],
        out_specs=pl.BlockSpec((None, rows, d), lambda b, i: (b, i, 0)),
        scratch_shapes=[
            pltpu.VMEM((SUBLANES, rows + HALO, c), F32),
            pltpu.VMEM((GLA_HEADS, dv // GLA_HEADS, 2 * dk // GLA_HEADS), F32),
        ],
        compiler_params=pltpu.CompilerParams(
            dimension_semantics=("arbitrary", "arbitrary"),
            vmem_limit_bytes=V7X_VMEM_LIMIT_BYTES),
        name="mixer",
    )(h, *[p[n] for n in names])


def _xattn_kernel(h_ref, wq_ref, kv_ref, wo_ref, lg_ref, lb_ref, o_ref, *, alpha):
    d = h_ref.shape[1]
    hd = d // X_HEADS
    h = h_ref[...]
    q = (_dot(h.astype(BF16), wq_ref[...]) * (hd ** -0.5)).astype(BF16)
    outs = []
    for i in range(X_HEADS):
        sl = slice(i * hd, (i + 1) * hd)
        s = _dot_nt(q[:, sl], kv_ref[:, sl])
        e = jnp.exp(s - jnp.max(s, axis=-1, keepdims=True))
        p = e * (1.0 / jnp.sum(e, axis=-1, keepdims=True))
        outs.append(_dot(p.astype(BF16), kv_ref[:, d + i * hd:d + (i + 1) * hd]).astype(BF16))
    xa = _dot(jnp.concatenate(outs, axis=-1), wo_ref[...])
    o_ref[...] = _layer_norm(alpha * h + xa, lg_ref[...], lb_ref[...])


def _xattn(h, kv, layer, p, alpha):
    bsz, t, d = h.shape
    rows = ATT_ROWS
    m = kv.shape[2]
    names = ["w_xq", "w_xo", "ln2_g", "ln2_b"]
    specs = {n: _const_spec(p[n].shape[1:], layer) for n in names}
    return pl.pallas_call(
        functools.partial(_xattn_kernel, alpha=alpha),
        out_shape=jax.ShapeDtypeStruct((bsz, t, d), F32),
        grid=(bsz, t // rows),
        in_specs=[pl.BlockSpec((None, rows, d), lambda b, i: (b, i, 0)),
                  specs["w_xq"],
                  pl.BlockSpec((None, None, m, 2 * d), lambda b, i: (layer, b, 0, 0)),
                  specs["w_xo"], specs["ln2_g"], specs["ln2_b"]],
        out_specs=pl.BlockSpec((None, rows, d), lambda b, i: (b, i, 0)),
        compiler_params=pltpu.CompilerParams(
            dimension_semantics=("parallel", "parallel"),
            vmem_limit_bytes=V7X_VMEM_LIMIT_BYTES),
        name="xattn",
    )(h, p["w_xq"], kv, p["w_xo"], p["ln2_g"], p["ln2_b"])


def _ffn_kernel(h_ref, w1_ref, w2_ref, lg_ref, lb_ref, o_ref, *, alpha, chunks):
    h = h_ref[...]
    h16 = h.astype(BF16)
    dff = w2_ref.shape[0]
    ff = None
    start = 0
    for width in chunks:
        g = _dot(h16, w1_ref[:, start:start + width])
        u = _dot(h16, w1_ref[:, dff + start:dff + start + width])
        part = _dot((_silu(g) * u).astype(BF16), w2_ref[start:start + width, :])
        ff = part if ff is None else ff + part
        start += width
    o_ref[...] = _layer_norm(alpha * h + ff, lg_ref[...], lb_ref[...])


def _ffn_chunks(dff, mxu_cols=2 * LANES, target=3):
    tiles = dff // mxu_cols
    assert tiles * mxu_cols == dff
    out = []
    while tiles > 0:
        take = min(target, tiles)
        out.append(take * mxu_cols)
        tiles -= take
    return tuple(out)


def _ffn(h, layer, p, alpha):
    bsz, t, d = h.shape
    rows = FFN_ROWS
    dff = p["w_ffn_out"].shape[1]
    names = ["w_ffn_in", "w_ffn_out", "ln3_g", "ln3_b"]
    return pl.pallas_call(
        functools.partial(_ffn_kernel, alpha=alpha, chunks=_ffn_chunks(dff)),
        out_shape=jax.ShapeDtypeStruct((bsz, t, d), F32),
        grid=(bsz, t // rows),
        in_specs=[pl.BlockSpec((None, rows, d), lambda b, i: (b, i, 0))]
        + [_const_spec(p[n].shape[1:], layer) for n in names],
        out_specs=pl.BlockSpec((None, rows, d), lambda b, i: (b, i, 0)),
        compiler_params=pltpu.CompilerParams(
            dimension_semantics=("parallel", "parallel"),
            vmem_limit_bytes=V7X_VMEM_LIMIT_BYTES),
        name="ffn",
    )(h, *[p[n] for n in names])


def kernel(x, mem, ln0_g, ln0_b, w_in, w_a2, b_a, conv_w, conv_b, conv_ln_g, conv_ln_b, gla_norm_g,
           w_mix_out, ln1_g, ln1_b, w_xq, w_xkv, w_xo, ln2_g, ln2_b, w_ffn_in, w_ffn_out, ln3_g,
           ln3_b):
    bsz, t, d = x.shape
    depth = w_in.shape[0]
    rank = w_a2.shape[1]
    alpha = (2 * depth) ** 0.25
    n_main = w_in.shape[2] - rank

    row = lambda a: a.reshape(depth, 1, a.shape[-1])
    p = {
        "w_main": w_in[:, :, :n_main].astype(BF16),
        "w_al": jnp.pad(w_in[:, :, n_main:], ((0, 0), (0, 0), (0, LANES - rank))).astype(BF16),
        "w_a2": jnp.pad(w_a2, ((0, 0), (0, LANES - rank), (0, 0))).astype(BF16),
        "w_mix_out": w_mix_out.astype(BF16),
        "w_xq": w_xq.astype(BF16),
        "w_xo": w_xo.astype(BF16),
        "w_ffn_in": w_ffn_in.astype(BF16),
        "w_ffn_out": w_ffn_out.astype(BF16),
        "conv_w": conv_w,
        "b_a": row(b_a), "conv_b": row(conv_b), "conv_ln_g": row(conv_ln_g),
        "conv_ln_b": row(conv_ln_b), "gla_norm_g": row(gla_norm_g),
        "ln1_g": row(ln1_g), "ln1_b": row(ln1_b), "ln2_g": row(ln2_g), "ln2_b": row(ln2_b),
        "ln3_g": row(ln3_g), "ln3_b": row(ln3_b),
    }
    kv = _memory_kv(mem, w_xkv.astype(BF16))
    h = _entry_layer_norm(x.reshape(bsz * t, d), ln0_g, ln0_b).reshape(bsz, t, d)
    for layer in range(depth):
        h = _mixer(h, layer, p, alpha)
        h = _xattn(h, kv, layer, p, alpha)
        h = _ffn(h, layer, p, alpha)
    return h
```

```python
import functools

import jax
import jax.numpy as jnp
from jax import lax
from jax.experimental import pallas as pl
from jax.experimental.pallas import tpu as pltpu

F32 = jnp.float32
BF16 = jnp.bfloat16

LN_EPS = 1e-5
GATE_TAU = 16.0
GLA_CHUNK = 64
GLA_HEADS = 4
X_HEADS = 4

LANES = 128
SUBLANES = 8
V7X_VMEM_LIMIT_BYTES = 60000 * 1024

MIX_ROWS = 512
FFN_ROWS = 512
LN_ROWS = 1024
CONV_ROWS = 32
HALO = 32

_NT = (((1,), (1,)), ((), ()))
_TN = (((0,), (0,)), ((), ()))


def _dot(a, b):
    return jnp.dot(a, b, preferred_element_type=F32)


def _dot_nt(a, b):
    return lax.dot_general(a, b, _NT, preferred_element_type=F32)


def _dot_tn(a, b):
    return lax.dot_general(a, b, _TN, preferred_element_type=F32)


def _layer_norm(x, g, b):
    mu = jnp.mean(x, axis=-1, keepdims=True)
    xc = x - mu
    var = jnp.mean(xc * xc, axis=-1, keepdims=True)
    return xc * lax.rsqrt(var + LN_EPS) * g + b


def _sigmoid(x):
    return 0.5 * jnp.tanh(0.5 * x) + 0.5


def _silu(x):
    half = 0.5 * x
    return half * jnp.tanh(half) + half


def _log_sigmoid(x):
    return jnp.minimum(x, 0.0) - jnp.log(1.0 + jnp.exp(-jnp.abs(x)))


def _ln_kernel(x_ref, g_ref, b_ref, o_ref):
    o_ref[...] = _layer_norm(x_ref[...], g_ref[...], b_ref[...])


def _entry_layer_norm(x2, g, b):
    n, d = x2.shape
    return pl.pallas_call(
        _ln_kernel,
        out_shape=jax.ShapeDtypeStruct((n, d), F32),
        grid=(n // LN_ROWS,),
        in_specs=[pl.BlockSpec((LN_ROWS, d), lambda i: (i, 0)),
                  pl.BlockSpec((1, d), lambda i: (0, 0)),
                  pl.BlockSpec((1, d), lambda i: (0, 0))],
        out_specs=pl.BlockSpec((LN_ROWS, d), lambda i: (i, 0)),
        compiler_params=pltpu.CompilerParams(dimension_semantics=("parallel",)),
        name="entry_ln",
    )(x2, g.reshape(1, d), b.reshape(1, d))


def _kv_kernel(mem_ref, w_ref, o_ref):
    o_ref[...] = _dot(mem_ref[...].astype(BF16), w_ref[...]).astype(BF16)


def _memory_kv(mem, w_xkv16):
    bsz, m, d = mem.shape
    depth, _, n = w_xkv16.shape
    return pl.pallas_call(
        _kv_kernel,
        out_shape=jax.ShapeDtypeStruct((depth, bsz, m, n), BF16),
        grid=(depth, bsz),
        in_specs=[pl.BlockSpec((None, m, d), lambda l, b: (b, 0, 0)),
                  pl.BlockSpec((None, d, n), lambda l, b: (l, 0, 0))],
        out_specs=pl.BlockSpec((None, None, m, n), lambda l, b: (l, b, 0, 0)),
        compiler_params=pltpu.CompilerParams(dimension_semantics=("arbitrary", "arbitrary")),
        name="memory_kv",
    )(mem, w_xkv16)


def _interleave(*streams):
    for rnd in range(max(first + len(s) for first, s in streams)):
        for first, s in streams:
            if 0 <= rnd - first < len(s):
                s[rnd - first]()


def _mix_att_kernel(hf_ref, hb_ref, w_in_ref, w_al_ref, w_a2_ref, b_a_ref, cw_ref, cb_ref, cg_ref,
                    cbeta_ref, gn_ref, w_out_ref, l1g_ref, l1b_ref, wq_ref, kv_ref, wo_ref, l2g_ref,
                    l2b_ref, o_ref, u_s, st_s, xo_s, *, alpha, tiles_per_seq, taps, conv_dim,
                    key_dim, val_dim):
    rows, d = hf_ref.shape
    c = conv_dim
    dk = key_dim
    dv = val_dim
    head_v = dv // GLA_HEADS
    head_k = dk // GLA_HEADS
    slab = 2 * head_k
    n_u = rows + HALO
    ch = GLA_CHUNK
    n_ch = rows // ch
    xh = d // X_HEADS
    mxu_n = 2 * LANES
    step = pl.program_id(0)

    @pl.when(step == 0)
    def _():
        xo_s[...] = jnp.zeros(xo_s.shape, xo_s.dtype)

    @pl.when(step % tiles_per_seq == 0)
    def _():
        u_s[0, 0:HALO, :] = jnp.zeros((HALO, c), F32)
        st_s[...] = jnp.zeros(st_s.shape, F32)

    val = {}

    h16 = hf_ref[...].astype(BF16)
    o_q = 2 * c
    o_v = o_q + 2 * dk
    o_r = o_v + dv
    n_vslab = dv // mxu_n

    ca = _dot(h16, w_in_ref[:, 0:c])
    cgate = _dot(h16, w_in_ref[:, c:2 * c])
    u_s[0, HALO:n_u, :] = ca * _sigmoid(cgate)
    first = HALO - (taps - 1)

    def shifted_copy(s):
        def run():
            u_s[s] = pltpu.roll(u_s[0], n_u - s, 0)
        return run

    def conv_block(rb):
        def run():
            r0 = rb * CONV_ROWS
            acc = None
            for k in range(taps):
                a, s = divmod(first + k, SUBLANES)
                lo = r0 + SUBLANES * a
                term = cw_ref[k:k + 1, :] * u_s[s, lo:lo + CONV_ROWS, :]
                acc = term if acc is None else acc + term
            val["y", rb] = acc
        return run

    def proj(name, w_ref, start, width):
        def run():
            val[name] = _dot(h16, w_ref[:, start:start + width])
        return run

    def out_proj(i):
        def run():
            val["mix", i] = _dot(xo_s[...], w_out_ref[:, i * mxu_n:(i + 1) * mxu_n])
        return run

    vpu_pieces = ([shifted_copy(s) for s in range(1, SUBLANES)]
                  + [conv_block(rb) for rb in range(rows // CONV_ROWS)])
    n_out = d // mxu_n
    mxu_pieces = ([out_proj(i) for i in range(n_out)]
                  + [proj("q", w_in_ref, o_q, dk), proj("k", w_in_ref, o_q + dk, dk),
                     proj("a_low", w_al_ref, 0, LANES)]
                  + [proj(("v", i), w_in_ref, o_v + i * mxu_n, mxu_n) for i in range(n_vslab)]
                  + [proj(("r", i), w_in_ref, o_r + i * mxu_n, mxu_n) for i in range(n_vslab)])
    gla_inputs_ready = n_out + 3 + n_vslab

    att = []

    def ln1():
        mix = jnp.concatenate([val["mix", i] for i in range(d // mxu_n)], axis=-1)
        val["h1"] = _layer_norm(alpha * hb_ref[...] + mix, l1g_ref[...], l1b_ref[...])
        val["h1_16"] = val["h1"].astype(BF16)
    att.append(ln1)

    def att_head(i):
        def run():
            sl = slice(i * xh, (i + 1) * xh)
            qx = (_dot(val["h1_16"], wq_ref[:, sl]) * (xh ** -0.5)).astype(BF16)
            s = _dot_nt(qx, kv_ref[:, sl])
            e = jnp.exp(s - jnp.max(s, axis=-1, keepdims=True))
            p = e * (1.0 / jnp.sum(e, axis=-1, keepdims=True))
            val["att", i] = _dot(p.astype(BF16), kv_ref[:, d + i * xh:d + (i + 1) * xh]).astype(BF16)
        return run
    att += [att_head(i) for i in range(X_HEADS)]

    def att_out(i):
        def run():
            if i == 0:
                val["att_cat"] = jnp.concatenate([val["att", j] for j in range(X_HEADS)], axis=-1)
            val["xa", i] = _dot(val["att_cat"], wo_ref[:, i * mxu_n:(i + 1) * mxu_n])
        return run
    att += [att_out(i) for i in range(d // mxu_n)]

    def ln2():
        xa = jnp.concatenate([val["xa", i] for i in range(d // mxu_n)], axis=-1)
        o_ref[...] = _layer_norm(alpha * val["h1"] + xa, l2g_ref[...], l2b_ref[...])
    att.append(ln2)

    gla = []

    def gla_setup():
        z = _dot(val["a_low"].astype(BF16), w_a2_ref[...]) + b_a_ref[...]
        g = _log_sigmoid(z) * (1.0 / GATE_TAU)
        val["g_hi"] = g.astype(BF16)
        val["g_lo"] = (g - val["g_hi"].astype(F32)).astype(BF16)
        val["q"] = val["q"] * (head_k ** -0.5)
        val["v16"] = jnp.concatenate([val["v", i] for i in range(n_vslab)], axis=-1).astype(BF16)
    gla.append(gla_setup)

    row_i = lax.broadcasted_iota(jnp.int32, (ch, ch), 0)
    col_i = lax.broadcasted_iota(jnp.int32, (ch, ch), 1)
    causal = row_i >= col_i
    tri16 = causal.astype(BF16)
    lane_i = lax.broadcasted_iota(jnp.int32, (ch, slab), 1)
    own = [(lane_i >= hh * head_k) & (lane_i < (hh + 1) * head_k) for hh in range(2)]
    qm = {}
    o_intra = {}
    upd = {}
    decay = {}

    def gla_chunk(ci):
        def run():
            rs = slice(ci * ch, (ci + 1) * ch)
            bcum = _dot(tri16, val["g_hi"][rs]) + _dot(tri16, val["g_lo"][rs])
            b_last = bcum[ch - 1:ch, :]
            qe = val["q"][rs] * jnp.exp(bcum)
            ke = (val["k"][rs] * jnp.exp(-bcum)).astype(BF16)
            kd = (val["k"][rs] * jnp.exp(b_last - bcum)).astype(BF16)
            decay[ci] = jnp.exp(b_last)
            for hd in range(GLA_HEADS):
                j, hh = divmod(hd, 2)
                sl = slice(slab * j, slab * (j + 1))
                qm[ci, hd] = jnp.where(own[hh], qe[:, sl], 0.0).astype(BF16)
                a = jnp.where(causal, _dot_nt(qm[ci, hd], ke[:, sl]), 0.0)
                v_h = val["v16"][rs, hd * head_v:(hd + 1) * head_v]
                o_intra[ci, hd] = _dot(a.astype(BF16), v_h)
                upd[ci, hd] = _dot_tn(v_h, kd[:, sl])
        return run
    gla += [gla_chunk(ci) for ci in range(n_ch)]

    def gla_head(hd):
        def run():
            j = hd // 2
            sl = slice(slab * j, slab * (j + 1))
            st = st_s[hd]
            outs = []
            for ci in range(n_ch):
                outs.append(o_intra[ci, hd] + _dot_nt(qm[ci, hd], st.astype(BF16)))
                st = st * decay[ci][:, sl] + upd[ci, hd]
            st_s[hd] = st
            o = jnp.concatenate(outs, axis=0)
            ms = jnp.mean(o * o, axis=-1, keepdims=True)
            r = val["r", hd // 2][:, (hd % 2) * head_v:(hd % 2 + 1) * head_v]
            val["gla", hd] = (o * lax.rsqrt(ms + LN_EPS) * gn_ref[...] * _silu(r)).astype(BF16)
        return run
    gla += [gla_head(hd) for hd in range(GLA_HEADS)]

    _interleave((0, vpu_pieces), (0, mxu_pieces), (n_out, att), (gla_inputs_ready, gla))
    u_s[0, 0:HALO, :] = u_s[0, rows:n_u, :]
    y = jnp.concatenate([val["y", rb] for rb in range(rows // CONV_ROWS)], axis=0)
    y = _layer_norm(y + cb_ref[...], cg_ref[...], cbeta_ref[...])
    xo_s[...] = jnp.concatenate(
        [_silu(y).astype(BF16)] + [val["gla", hd] for hd in range(GLA_HEADS)], axis=-1)


def _const_spec(shape, layer):
    nd = len(shape)
    return pl.BlockSpec((None,) + tuple(shape), lambda *_: (layer,) + (0,) * nd,
                        pipeline_mode=pl.Buffered(1))


def _mix_att(h2, kv, layer, p, alpha, seq_len):
    n, d = h2.shape
    rows = MIX_ROWS
    tiles = n // rows
    tiles_per_seq = seq_len // rows
    taps, c = p["conv_w"].shape[1:]
    dk = p["w_a2"].shape[2]
    dv = (p["w_main"].shape[2] - 2 * c - 2 * dk) // 2
    m = kv.shape[2]
    kern = functools.partial(_mix_att_kernel, alpha=alpha, tiles_per_seq=tiles_per_seq, taps=taps,
                             conv_dim=c, key_dim=dk, val_dim=dv)
    front = lambda s: (jnp.minimum(s, tiles - 1), 0)
    back = lambda s: (jnp.maximum(s - 1, 0), 0)
    names_a = ["w_main", "w_al", "w_a2", "b_a", "conv_w", "conv_b", "conv_ln_g", "conv_ln_b",
               "gla_norm_g", "w_mix_out", "ln1_g", "ln1_b", "w_xq"]
    names_b = ["w_xo", "ln2_g", "ln2_b"]
    kv_spec = pl.BlockSpec((None, None, m, 2 * d),
                           lambda s: (layer, jnp.maximum(s - 1, 0) // tiles_per_seq, 0, 0))
    return pl.pallas_call(
        kern,
        out_shape=jax.ShapeDtypeStruct((n, d), F32),
        grid=(tiles + 1,),
        in_specs=[pl.BlockSpec((rows, d), front), pl.BlockSpec((rows, d), back)]
        + [_const_spec(p[k].shape[1:], layer) for k in names_a] + [kv_spec]
        + [_const_spec(p[k].shape[1:], layer) for k in names_b],
        out_specs=pl.BlockSpec((rows, d), back),
        scratch_shapes=[
            pltpu.VMEM((SUBLANES, rows + HALO, c), F32),
            pltpu.VMEM((GLA_HEADS, dv // GLA_HEADS, 2 * dk // GLA_HEADS), F32),
            pltpu.VMEM((rows, c + dv), BF16),
        ],
        compiler_params=pltpu.CompilerParams(
            dimension_semantics=("arbitrary",),
            vmem_limit_bytes=V7X_VMEM_LIMIT_BYTES),
        name="mix_att",
    )(h2, h2, *[p[k] for k in names_a], kv, *[p[k] for k in names_b])


def _ffn_kernel(h_ref, w1_ref, w2_ref, lg_ref, lb_ref, o_ref, *, alpha, chunks):
    h = h_ref[...]
    h16 = h.astype(BF16)
    dff = w2_ref.shape[0]
    ff = None
    start = 0
    for width in chunks:
        g = _dot(h16, w1_ref[:, start:start + width])
        u = _dot(h16, w1_ref[:, dff + start:dff + start + width])
        part = _dot((_silu(g) * u).astype(BF16), w2_ref[start:start + width, :])
        ff = part if ff is None else ff + part
        start += width
    o_ref[...] = _layer_norm(alpha * h + ff, lg_ref[...], lb_ref[...])


def _ffn_chunks(dff, mxu_cols=2 * LANES, target=3):
    tiles = dff // mxu_cols
    assert tiles * mxu_cols == dff
    out = []
    while tiles > 0:
        take = min(target, tiles)
        out.append(take * mxu_cols)
        tiles -= take
    return tuple(out)


def _ffn(h2, layer, p, alpha):
    n, d = h2.shape
    rows = FFN_ROWS
    dff = p["w_ffn_out"].shape[1]
    names = ["w_ffn_in", "w_ffn_out", "ln3_g", "ln3_b"]
    return pl.pallas_call(
        functools.partial(_ffn_kernel, alpha=alpha, chunks=_ffn_chunks(dff)),
        out_shape=jax.ShapeDtypeStruct((n, d), F32),
        grid=(n // rows,),
        in_specs=[pl.BlockSpec((rows, d), lambda i: (i, 0))]
        + [_const_spec(p[k].shape[1:], layer) for k in names],
        out_specs=pl.BlockSpec((rows, d), lambda i: (i, 0)),
        compiler_params=pltpu.CompilerParams(
            dimension_semantics=("parallel",),
            vmem_limit_bytes=V7X_VMEM_LIMIT_BYTES),
        name="ffn",
    )(h2, *[p[k] for k in names])


def kernel(x, mem, ln0_g, ln0_b, w_in, w_a2, b_a, conv_w, conv_b, conv_ln_g, conv_ln_b, gla_norm_g,
           w_mix_out, ln1_g, ln1_b, w_xq, w_xkv, w_xo, ln2_g, ln2_b, w_ffn_in, w_ffn_out, ln3_g,
           ln3_b):
    bsz, t, d = x.shape
    depth = w_in.shape[0]
    rank = w_a2.shape[1]
    alpha = (2 * depth) ** 0.25
    n_main = w_in.shape[2] - rank

    row = lambda a: a.reshape(depth, 1, a.shape[-1])
    p = {
        "w_main": w_in[:, :, :n_main].astype(BF16),
        "w_al": jnp.pad(w_in[:, :, n_main:], ((0, 0), (0, 0), (0, LANES - rank))).astype(BF16),
        "w_a2": jnp.pad(w_a2, ((0, 0), (0, LANES - rank), (0, 0))).astype(BF16),
        "w_mix_out": w_mix_out.astype(BF16),
        "w_xq": w_xq.astype(BF16),
        "w_xo": w_xo.astype(BF16),
        "w_ffn_in": w_ffn_in.astype(BF16),
        "w_ffn_out": w_ffn_out.astype(BF16),
        "conv_w": conv_w,
        "b_a": row(b_a), "conv_b": row(conv_b), "conv_ln_g": row(conv_ln_g),
        "conv_ln_b": row(conv_ln_b), "gla_norm_g": row(gla_norm_g),
        "ln1_g": row(ln1_g), "ln1_b": row(ln1_b), "ln2_g": row(ln2_g), "ln2_b": row(ln2_b),
        "ln3_g": row(ln3_g), "ln3_b": row(ln3_b),
    }
    kv = _memory_kv(mem, w_xkv.astype(BF16))
    h = _entry_layer_norm(x.reshape(bsz * t, d), ln0_g, ln0_b)
    for layer in range(depth):
        h = _mix_att(h, kv, layer, p, alpha, t)
        h = _ffn(h, layer, p, alpha)
    return h.reshape(bsz, t, d)
```

```python
import functools

import jax
import jax.numpy as jnp
from jax import lax
from jax.experimental import pallas as pl
from jax.experimental.pallas import tpu as pltpu

F32 = jnp.float32
BF16 = jnp.bfloat16

LN_EPS = 1e-5
GATE_TAU = 16.0
GLA_CHUNK = 64
GLA_HEADS = 4
X_HEADS = 4

LANES = 128
SUBLANES = 8
V7X_VMEM_LIMIT_BYTES = 60000 * 1024

MIX_ROWS = 512
FFN_ROWS = 1024
LN_ROWS = 1024
CONV_ROWS = 32
HALO = 32

_NT = (((1,), (1,)), ((), ()))
_TN = (((0,), (0,)), ((), ()))


def _dot(a, b):
    return jnp.dot(a, b, preferred_element_type=F32)


def _dot_nt(a, b):
    return lax.dot_general(a, b, _NT, preferred_element_type=F32)


def _dot_tn(a, b):
    return lax.dot_general(a, b, _TN, preferred_element_type=F32)


def _layer_norm(x, g, b):
    mu = jnp.mean(x, axis=-1, keepdims=True)
    xc = x - mu
    var = jnp.mean(xc * xc, axis=-1, keepdims=True)
    return xc * lax.rsqrt(var + LN_EPS) * g + b


def _sigmoid(x):
    return 0.5 * jnp.tanh(0.5 * x) + 0.5


def _silu(x):
    half = 0.5 * x
    return half * jnp.tanh(half) + half


def _log_sigmoid(x):
    return jnp.minimum(x, 0.0) - jnp.log(1.0 + jnp.exp(-jnp.abs(x)))


def _ln_kernel(x_ref, g_ref, b_ref, o_ref):
    o_ref[...] = _layer_norm(x_ref[...], g_ref[...], b_ref[...])


def _entry_layer_norm(x2, g, b):
    n, d = x2.shape
    return pl.pallas_call(
        _ln_kernel,
        out_shape=jax.ShapeDtypeStruct((n, d), F32),
        grid=(n // LN_ROWS,),
        in_specs=[pl.BlockSpec((LN_ROWS, d), lambda i: (i, 0)),
                  pl.BlockSpec((1, d), lambda i: (0, 0)),
                  pl.BlockSpec((1, d), lambda i: (0, 0))],
        out_specs=pl.BlockSpec((LN_ROWS, d), lambda i: (i, 0)),
        compiler_params=pltpu.CompilerParams(dimension_semantics=("parallel",)),
        name="entry_ln",
    )(x2, g.reshape(1, d), b.reshape(1, d))


def _kv_kernel(mem_ref, w_ref, o_ref):
    o_ref[...] = _dot(mem_ref[...].astype(BF16), w_ref[...]).astype(BF16)


def _memory_kv(mem, w_xkv16):
    bsz, m, d = mem.shape
    depth, _, n = w_xkv16.shape
    return pl.pallas_call(
        _kv_kernel,
        out_shape=jax.ShapeDtypeStruct((depth, bsz, m, n), BF16),
        grid=(depth, bsz),
        in_specs=[pl.BlockSpec((None, m, d), lambda l, b: (b, 0, 0)),
                  pl.BlockSpec((None, d, n), lambda l, b: (l, 0, 0))],
        out_specs=pl.BlockSpec((None, None, m, n), lambda l, b: (l, b, 0, 0)),
        compiler_params=pltpu.CompilerParams(dimension_semantics=("arbitrary", "arbitrary")),
        name="memory_kv",
    )(mem, w_xkv16)


def _interleave(*streams):
    for rnd in range(max(first + len(s) for first, s in streams)):
        for first, s in streams:
            if 0 <= rnd - first < len(s):
                s[rnd - first]()


def _mix_att_kernel(hf_ref, hb_ref, w_in_ref, w_al_ref, w_a2_ref, b_a_ref, cw_ref, cb_ref, cg_ref,
                    cbeta_ref, gn_ref, w_out_ref, l1g_ref, l1b_ref, wq_ref, kv_ref, wo_ref, l2g_ref,
                    l2b_ref, o_ref, u_s, st_s, xo_s, *, alpha, tiles_per_seq, taps, conv_dim,
                    key_dim, val_dim):
    rows, d = hf_ref.shape
    c = conv_dim
    dk = key_dim
    dv = val_dim
    head_v = dv // GLA_HEADS
    head_k = dk // GLA_HEADS
    slab = 2 * head_k
    n_u = rows + HALO
    ch = GLA_CHUNK
    n_ch = rows // ch
    xh = d // X_HEADS
    mxu_n = 2 * LANES
    step = pl.program_id(0)

    @pl.when(step == 0)
    def _():
        xo_s[...] = jnp.zeros(xo_s.shape, xo_s.dtype)

    @pl.when(step % tiles_per_seq == 0)
    def _():
        u_s[0, 0:HALO, :] = jnp.zeros((HALO, c), F32)
        st_s[...] = jnp.zeros(st_s.shape, F32)

    val = {}

    h16 = hf_ref[...].astype(BF16)
    o_q = 2 * c
    o_v = o_q + 2 * dk
    o_r = o_v + dv
    n_vslab = dv // mxu_n

    ca = _dot(h16, w_in_ref[:, 0:c])
    cgate = _dot(h16, w_in_ref[:, c:2 * c])
    u_s[0, HALO:n_u, :] = ca * _sigmoid(cgate)
    first = HALO - (taps - 1)

    n_grp = n_u // SUBLANES
    sub_i = lax.broadcasted_iota(jnp.int32, (n_grp - 1, SUBLANES, c), 1)

    def shifted_copy(s):
        def run():
            x = u_s[0].reshape(n_grp, SUBLANES, c)
            mixed = jnp.where(sub_i >= s, x[:-1], x[1:])
            out = pltpu.roll(mixed, SUBLANES - s, 1)
            u_s[s, 0:n_u - SUBLANES, :] = out.reshape(n_u - SUBLANES, c)
        return run

    grp_per_block = CONV_ROWS // SUBLANES

    def conv_block(rb):
        def run():
            r0 = rb * CONV_ROWS
            acc = None
            for k in range(taps):
                a, s = divmod(first + k, SUBLANES)
                lo = r0 + SUBLANES * a
                x = u_s[s, lo:lo + CONV_ROWS, :].reshape(grp_per_block, SUBLANES, c)
                term = cw_ref[k] * x
                acc = term if acc is None else acc + term
            val["y", rb] = acc.reshape(CONV_ROWS, c)
        return run

    def proj(name, w_ref, start, width, post=lambda t: t):
        def run():
            val[name] = post(_dot(h16, w_ref[:, start:start + width]))
        return run

    vpu_pieces = ([shifted_copy(s) for s in range(1, SUBLANES)]
                  + [conv_block(rb) for rb in range(rows // CONV_ROWS)])
    def out_proj():
        val["mix"] = _dot(xo_s[...], w_out_ref[...])

    mxu_pieces = ([out_proj]
                  + [proj("q", w_in_ref, o_q, dk, lambda t: t * (head_k ** -0.5)),
                     proj("k", w_in_ref, o_q + dk, dk),
                     proj("a_low", w_al_ref, 0, LANES, lambda t: t.astype(BF16))]
                  + [proj(("v", i), w_in_ref, o_v + i * mxu_n, mxu_n, lambda t: t.astype(BF16))
                     for i in range(n_vslab)]
                  + [proj(("r", i), w_in_ref, o_r + i * mxu_n, mxu_n) for i in range(n_vslab)])
    gla_inputs_ready = 1 + 3 + n_vslab

    def ln1():
        val["h1"] = _layer_norm(alpha * hb_ref[...] + val["mix"], l1g_ref[...], l1b_ref[...])
        val["h1_16"] = val["h1"].astype(BF16)

    def att_q(i):
        def run():
            sl = slice(i * xh, (i + 1) * xh)
            val["qx", i] = (_dot(val["h1_16"], wq_ref[:, sl]) * (xh ** -0.5)).astype(BF16)
        return run

    def att_p(i):
        def run():
            s = _dot_nt(val["qx", i], kv_ref[:, i * xh:(i + 1) * xh])
            e = jnp.exp(s - jnp.max(s, axis=-1, keepdims=True))
            val["p", i] = (e * (1.0 / jnp.sum(e, axis=-1, keepdims=True))).astype(BF16)
        return run

    def att_v(i):
        def run():
            val["att", i] = _dot(val["p", i], kv_ref[:, d + i * xh:d + (i + 1) * xh]).astype(BF16)
        return run

    def att_out(i):
        def run():
            if i == 0:
                val["att_cat"] = jnp.concatenate([val["att", j] for j in range(X_HEADS)], axis=-1)
            val["xa", i] = _dot(val["att_cat"], wo_ref[:, i * mxu_n:(i + 1) * mxu_n])
        return run

    def ln2():
        xa = jnp.concatenate([val["xa", i] for i in range(d // mxu_n)], axis=-1)
        o_ref[...] = _layer_norm(alpha * val["h1"] + xa, l2g_ref[...], l2b_ref[...])

    att_start = 2
    att_streams = [(att_start + i, [att_q(i), att_p(i), att_v(i)]) for i in range(X_HEADS)]
    att_done = att_start + X_HEADS + 2
    out_stream = (att_done, [att_out(i) for i in range(d // mxu_n)] + [ln2])

    def gla_setup():
        z = _dot(val["a_low"], w_a2_ref[...]) + b_a_ref[...]
        g = _log_sigmoid(z) * (1.0 / GATE_TAU)
        val["g_hi"] = g.astype(BF16)
        val["g_lo"] = (g - val["g_hi"].astype(F32)).astype(BF16)
        val["v16"] = jnp.concatenate([val["v", i] for i in range(n_vslab)], axis=-1)

    row_i = lax.broadcasted_iota(jnp.int32, (ch, ch), 0)
    col_i = lax.broadcasted_iota(jnp.int32, (ch, ch), 1)
    causal = row_i >= col_i
    tri16 = causal.astype(BF16)
    lane_i = lax.broadcasted_iota(jnp.int32, (ch, slab), 1)
    own = [(lane_i >= hh * head_k) & (lane_i < (hh + 1) * head_k) for hh in range(2)]
    qm = {}
    o_intra = {}
    upd = {}
    decay = {}

    def gla_decay(ci):
        def run():
            rs = slice(ci * ch, (ci + 1) * ch)
            bcum = _dot(tri16, val["g_hi"][rs]) + _dot(tri16, val["g_lo"][rs])
            b_last = bcum[ch - 1:ch, :]
            qe = val["q"][rs] * jnp.exp(bcum)
            val["ke", ci] = (val["k"][rs] * jnp.exp(-bcum)).astype(BF16)
            val["kd", ci] = (val["k"][rs] * jnp.exp(b_last - bcum)).astype(BF16)
            decay[ci] = jnp.exp(b_last)
            for hd in range(GLA_HEADS):
                j, hh = divmod(hd, 2)
                sl = slice(slab * j, slab * (j + 1))
                qm[ci, hd] = jnp.where(own[hh], qe[:, sl], 0.0).astype(BF16)
        return run

    def gla_intra(ci):
        def run():
            rs = slice(ci * ch, (ci + 1) * ch)
            for hd in range(GLA_HEADS):
                j = hd // 2
                sl = slice(slab * j, slab * (j + 1))
                a = jnp.where(causal, _dot_nt(qm[ci, hd], val["ke", ci][:, sl]), 0.0)
                v_h = val["v16"][rs, hd * head_v:(hd + 1) * head_v]
                o_intra[ci, hd] = _dot(a.astype(BF16), v_h)
                upd[ci, hd] = _dot_tn(v_h, val["kd", ci][:, sl])
        return run

    def gla_head(hd):
        def run():
            j = hd // 2
            sl = slice(slab * j, slab * (j + 1))
            st = st_s[hd]
            outs = []
            for ci in range(n_ch):
                outs.append(o_intra[ci, hd] + _dot_nt(qm[ci, hd], st.astype(BF16)))
                st = st * decay[ci][:, sl] + upd[ci, hd]
            st_s[hd] = st
            o = jnp.concatenate(outs, axis=0)
            ms = jnp.mean(o * o, axis=-1, keepdims=True)
            r = val["r", hd // 2][:, (hd % 2) * head_v:(hd % 2 + 1) * head_v]
            val["gla", hd] = (o * lax.rsqrt(ms + LN_EPS) * gn_ref[...] * _silu(r)).astype(BF16)
        return run

    gla_streams = ([(gla_inputs_ready, [gla_setup])]
                   + [(gla_inputs_ready + 1 + ci, [gla_decay(ci), gla_intra(ci)]) for ci in range(n_ch)]
                   + [(gla_inputs_ready + n_ch + 3, [gla_head(hd) for hd in range(GLA_HEADS)])])

    _interleave((0, vpu_pieces), (0, mxu_pieces), (1, [ln1]), *att_streams, out_stream,
                *gla_streams)
    u_s[0, 0:HALO, :] = u_s[0, rows:n_u, :]
    y = jnp.concatenate([val["y", rb] for rb in range(rows // CONV_ROWS)], axis=0)
    y = _layer_norm(y + cb_ref[...], cg_ref[...], cbeta_ref[...])
    xo_s[...] = jnp.concatenate(
        [_silu(y).astype(BF16)] + [val["gla", hd] for hd in range(GLA_HEADS)], axis=-1)


def _const_spec(shape, layer):
    nd = len(shape)
    return pl.BlockSpec((None,) + tuple(shape), lambda *_: (layer,) + (0,) * nd,
                        pipeline_mode=pl.Buffered(1))


def _mix_att(h2, kv, layer, p, alpha, seq_len):
    n, d = h2.shape
    rows = MIX_ROWS
    tiles = n // rows
    tiles_per_seq = seq_len // rows
    taps, _, c = p["conv_w"].shape[1:]
    dk = p["w_a2"].shape[2]
    dv = (p["w_main"].shape[2] - 2 * c - 2 * dk) // 2
    m = kv.shape[2]
    kern = functools.partial(_mix_att_kernel, alpha=alpha, tiles_per_seq=tiles_per_seq, taps=taps,
                             conv_dim=c, key_dim=dk, val_dim=dv)
    front = lambda s: (jnp.minimum(s, tiles - 1), 0)
    back = lambda s: (jnp.maximum(s - 1, 0), 0)
    names_a = ["w_main", "w_al", "w_a2", "b_a", "conv_w", "conv_b", "conv_ln_g", "conv_ln_b",
               "gla_norm_g", "w_mix_out", "ln1_g", "ln1_b", "w_xq"]
    names_b = ["w_xo", "ln2_g", "ln2_b"]
    kv_spec = pl.BlockSpec((None, None, m, 2 * d),
                           lambda s: (layer, jnp.maximum(s - 1, 0) // tiles_per_seq, 0, 0))
    return pl.pallas_call(
        kern,
        out_shape=jax.ShapeDtypeStruct((n, d), F32),
        grid=(tiles + 1,),
        in_specs=[pl.BlockSpec((rows, d), front), pl.BlockSpec((rows, d), back)]
        + [_const_spec(p[k].shape[1:], layer) for k in names_a] + [kv_spec]
        + [_const_spec(p[k].shape[1:], layer) for k in names_b],
        out_specs=pl.BlockSpec((rows, d), back),
        scratch_shapes=[
            pltpu.VMEM((SUBLANES, rows + HALO, c), F32),
            pltpu.VMEM((GLA_HEADS, dv // GLA_HEADS, 2 * dk // GLA_HEADS), F32),
            pltpu.VMEM((rows, c + dv), BF16),
        ],
        compiler_params=pltpu.CompilerParams(
            dimension_semantics=("arbitrary",),
            vmem_limit_bytes=V7X_VMEM_LIMIT_BYTES),
        name="mix_att",
    )(h2, h2, *[p[k] for k in names_a], kv, *[p[k] for k in names_b])


def _ffn_kernel(h_ref, w1_ref, w2_ref, lg_ref, lb_ref, o_ref, *, alpha, chunks):
    h = h_ref[...]
    h16 = h.astype(BF16)
    dff = w2_ref.shape[0]
    ff = None
    start = 0
    for width in chunks:
        g = _dot(h16, w1_ref[:, start:start + width])
        u = _dot(h16, w1_ref[:, dff + start:dff + start + width])
        part = _dot((_silu(g) * u).astype(BF16), w2_ref[start:start + width, :])
        ff = part if ff is None else ff + part
        start += width
    o_ref[...] = _layer_norm(alpha * h + ff, lg_ref[...], lb_ref[...])


def _ffn_chunks(dff, mxu_cols=2 * LANES, target=3):
    tiles = dff // mxu_cols
    assert tiles * mxu_cols == dff
    out = []
    while tiles > 0:
        take = min(target, tiles)
        out.append(take * mxu_cols)
        tiles -= take
    return tuple(out)


def _ffn(h2, layer, p, alpha):
    n, d = h2.shape
    rows = FFN_ROWS
    dff = p["w_ffn_out"].shape[1]
    names = ["w_ffn_in", "w_ffn_out", "ln3_g", "ln3_b"]
    return pl.pallas_call(
        functools.partial(_ffn_kernel, alpha=alpha, chunks=_ffn_chunks(dff)),
        out_shape=jax.ShapeDtypeStruct((n, d), F32),
        grid=(n // rows,),
        in_specs=[pl.BlockSpec((rows, d), lambda i: (i, 0))]
        + [_const_spec(p[k].shape[1:], layer) for k in names],
        out_specs=pl.BlockSpec((rows, d), lambda i: (i, 0)),
        compiler_params=pltpu.CompilerParams(
            dimension_semantics=("parallel",),
            vmem_limit_bytes=V7X_VMEM_LIMIT_BYTES),
        name="ffn",
    )(h2, *[p[k] for k in names])


def kernel(x, mem, ln0_g, ln0_b, w_in, w_a2, b_a, conv_w, conv_b, conv_ln_g, conv_ln_b, gla_norm_g,
           w_mix_out, ln1_g, ln1_b, w_xq, w_xkv, w_xo, ln2_g, ln2_b, w_ffn_in, w_ffn_out, ln3_g,
           ln3_b):
    bsz, t, d = x.shape
    depth = w_in.shape[0]
    rank = w_a2.shape[1]
    alpha = (2 * depth) ** 0.25
    n_main = w_in.shape[2] - rank

    row = lambda a: a.reshape(depth, 1, a.shape[-1])
    p = {
        "w_main": w_in[:, :, :n_main].astype(BF16),
        "w_al": jnp.pad(w_in[:, :, n_main:], ((0, 0), (0, 0), (0, LANES - rank))).astype(BF16),
        "w_a2": jnp.pad(w_a2, ((0, 0), (0, LANES - rank), (0, 0))).astype(BF16),
        "w_mix_out": w_mix_out.astype(BF16),
        "w_xq": w_xq.astype(BF16),
        "w_xo": w_xo.astype(BF16),
        "w_ffn_in": w_ffn_in.astype(BF16),
        "w_ffn_out": w_ffn_out.astype(BF16),
        "conv_w": jnp.broadcast_to(conv_w[:, :, None, :],
                                   conv_w.shape[:2] + (SUBLANES, conv_w.shape[2])),
        "b_a": row(b_a), "conv_b": row(conv_b), "conv_ln_g": row(conv_ln_g),
        "conv_ln_b": row(conv_ln_b), "gla_norm_g": row(gla_norm_g),
        "ln1_g": row(ln1_g), "ln1_b": row(ln1_b), "ln2_g": row(ln2_g), "ln2_b": row(ln2_b),
        "ln3_g": row(ln3_g), "ln3_b": row(ln3_b),
    }
    kv = _memory_kv(mem, w_xkv.astype(BF16))
    h = _entry_layer_norm(x.reshape(bsz * t, d), ln0_g, ln0_b)
    for layer in range(depth):
        h = _mix_att(h, kv, layer, p, alpha, t)
        h = _ffn(h, layer, p, alpha)
    return h.reshape(bsz, t, d)
```

```python
import functools

import jax
import jax.numpy as jnp
from jax import lax
from jax.experimental import pallas as pl
from jax.experimental.pallas import tpu as pltpu

F32 = jnp.float32
BF16 = jnp.bfloat16

LN_EPS = 1e-5
GATE_TAU = 16.0
GLA_CHUNK = 64
GLA_HEADS = 4
X_HEADS = 4

LANES = 128
SUBLANES = 8
V7X_VMEM_LIMIT_BYTES = 60000 * 1024

MIX_ROWS = 512
FFN_ROWS = 1024
LN_ROWS = 1024
CONV_ROWS = 32
NORM_ROWS = 128
HALO = 32

_NT = (((1,), (1,)), ((), ()))
_TN = (((0,), (0,)), ((), ()))


def _dot(a, b):
    return jnp.dot(a, b, preferred_element_type=F32)


def _dot_nt(a, b):
    return lax.dot_general(a, b, _NT, preferred_element_type=F32)


def _dot_tn(a, b):
    return lax.dot_general(a, b, _TN, preferred_element_type=F32)


def _layer_norm(x, g, b):
    mu = jnp.mean(x, axis=-1, keepdims=True)
    xc = x - mu
    var = jnp.mean(xc * xc, axis=-1, keepdims=True)
    return xc * lax.rsqrt(var + LN_EPS) * g + b


def _sigmoid(x):
    return 0.5 * jnp.tanh(0.5 * x) + 0.5


def _silu(x):
    half = 0.5 * x
    return half * jnp.tanh(half) + half


def _log_sigmoid(x):
    return jnp.minimum(x, 0.0) - jnp.log(1.0 + jnp.exp(-jnp.abs(x)))


def _ln_kernel(x_ref, g_ref, b_ref, o_ref):
    o_ref[...] = _layer_norm(x_ref[...], g_ref[...], b_ref[...])


def _entry_layer_norm(x2, g, b):
    n, d = x2.shape
    return pl.pallas_call(
        _ln_kernel,
        out_shape=jax.ShapeDtypeStruct((n, d), F32),
        grid=(n // LN_ROWS,),
        in_specs=[pl.BlockSpec((LN_ROWS, d), lambda i: (i, 0)),
                  pl.BlockSpec((1, d), lambda i: (0, 0)),
                  pl.BlockSpec((1, d), lambda i: (0, 0))],
        out_specs=pl.BlockSpec((LN_ROWS, d), lambda i: (i, 0)),
        compiler_params=pltpu.CompilerParams(dimension_semantics=("parallel",)),
        name="entry_ln",
    )(x2, g.reshape(1, d), b.reshape(1, d))


def _kv_kernel(mem_ref, w_ref, o_ref):
    o_ref[...] = _dot(mem_ref[...].astype(BF16), w_ref[...]).astype(BF16)


def _memory_kv(mem, w_xkv16):
    bsz, m, d = mem.shape
    depth, _, n = w_xkv16.shape
    return pl.pallas_call(
        _kv_kernel,
        out_shape=jax.ShapeDtypeStruct((depth, bsz, m, n), BF16),
        grid=(depth, bsz),
        in_specs=[pl.BlockSpec((None, m, d), lambda l, b: (b, 0, 0)),
                  pl.BlockSpec((None, d, n), lambda l, b: (l, 0, 0))],
        out_specs=pl.BlockSpec((None, None, m, n), lambda l, b: (l, b, 0, 0)),
        compiler_params=pltpu.CompilerParams(dimension_semantics=("arbitrary", "arbitrary")),
        name="memory_kv",
    )(mem, w_xkv16)


def _interleave(*streams):
    for rnd in range(max(first + len(s) for first, s in streams)):
        for first, s in streams:
            if 0 <= rnd - first < len(s):
                s[rnd - first]()


def _mix_att_kernel(hf_ref, hb_ref, w_in_ref, w_al_ref, w_a2_ref, b_a_ref, cw_ref, cb_ref, cg_ref,
                    cbeta_ref, gn_ref, w_out_ref, l1g_ref, l1b_ref, wq_ref, kv_ref, wo_ref, l2g_ref,
                    l2b_ref, o_ref, u_s, st_s, xo_s, *, alpha, tiles_per_seq, taps, conv_dim,
                    key_dim, val_dim):
    rows, d = hf_ref.shape
    c = conv_dim
    dk = key_dim
    dv = val_dim
    head_v = dv // GLA_HEADS
    head_k = dk // GLA_HEADS
    slab = 2 * head_k
    n_u = rows + HALO
    ch = GLA_CHUNK
    n_ch = rows // ch
    xh = d // X_HEADS
    mxu_n = 2 * LANES
    step = pl.program_id(0)

    @pl.when(step == 0)
    def _():
        xo_s[...] = jnp.zeros(xo_s.shape, xo_s.dtype)

    @pl.when(step % tiles_per_seq == 0)
    def _():
        u_s[0, 0:HALO, :] = jnp.zeros((HALO, c), F32)
        st_s[...] = jnp.zeros(st_s.shape, F32)

    val = {}

    h16 = hf_ref[...].astype(BF16)
    o_q = 2 * c
    o_v = o_q + 2 * dk
    o_r = o_v + dv
    n_vslab = dv // mxu_n

    first = HALO - (taps - 1)
    n_grp = n_u // SUBLANES
    n_half = c // mxu_n
    sub_i = lax.broadcasted_iota(jnp.int32, (n_grp - 1, SUBLANES, mxu_n), 1)
    grp_per_block = CONV_ROWS // SUBLANES

    def glu(half):
        def run():
            lanes = slice(half * mxu_n, (half + 1) * mxu_n)
            ca = _dot(h16, w_in_ref[:, half * mxu_n:(half + 1) * mxu_n])
            cgate = _dot(h16, w_in_ref[:, c + half * mxu_n:c + (half + 1) * mxu_n])
            u_s[0, HALO:n_u, lanes] = ca * _sigmoid(cgate)
        return run

    def shifted_copy(s, half):
        def run():
            lanes = slice(half * mxu_n, (half + 1) * mxu_n)
            x = u_s[0, :, lanes].reshape(n_grp, SUBLANES, mxu_n)
            mixed = jnp.where(sub_i >= s, x[:-1], x[1:])
            out = pltpu.roll(mixed, SUBLANES - s, 1)
            u_s[s, 0:n_u - SUBLANES, lanes] = out.reshape(n_u - SUBLANES, mxu_n)
        return run

    def conv_block(rb, half):
        def run():
            lanes = slice(half * mxu_n, (half + 1) * mxu_n)
            r0 = rb * CONV_ROWS
            acc = None
            for k in range(taps):
                a, s = divmod(first + k, SUBLANES)
                lo = r0 + SUBLANES * a
                x = u_s[s, lo:lo + CONV_ROWS, lanes].reshape(grp_per_block, SUBLANES, mxu_n)
                term = cw_ref[k, :, lanes] * x
                acc = term if acc is None else acc + term
            val["y", rb, half] = acc.reshape(CONV_ROWS, mxu_n)
        return run

    def proj(name, w_ref, start, width, post=lambda t: t):
        def run():
            val[name] = post(_dot(h16, w_ref[:, start:start + width]))
        return run

    glu(0)()
    vpu_streams = [(half, [shifted_copy(s, half) for s in range(1, SUBLANES)]
                    + [conv_block(rb, half) for rb in range(rows // CONV_ROWS)])
                   for half in range(n_half)]
    last_copy = n_half - 1 + SUBLANES - 1
    def out_proj():
        val["mix"] = _dot(xo_s[...], w_out_ref[...])

    mxu_pieces = ([glu(half) for half in range(1, n_half)] + [out_proj]
                  + [proj("q", w_in_ref, o_q, dk, lambda t: t * (head_k ** -0.5)),
                     proj("k", w_in_ref, o_q + dk, dk),
                     proj("a_low", w_al_ref, 0, LANES, lambda t: t.astype(BF16))]
                  + [proj(("v", i), w_in_ref, o_v + i * mxu_n, mxu_n, lambda t: t.astype(BF16))
                     for i in range(n_vslab)]
                  + [proj(("r", i), w_in_ref, o_r + i * mxu_n, mxu_n) for i in range(n_vslab)])
    gla_inputs_ready = n_half + 3 + n_vslab

    def ln1():
        val["h1"] = _layer_norm(alpha * hb_ref[...] + val["mix"], l1g_ref[...], l1b_ref[...])
        val["h1_16"] = val["h1"].astype(BF16)

    def att_q(i):
        def run():
            sl = slice(i * xh, (i + 1) * xh)
            val["qx", i] = (_dot(val["h1_16"], wq_ref[:, sl]) * (xh ** -0.5)).astype(BF16)
        return run

    def att_p(i):
        def run():
            s = _dot_nt(val["qx", i], kv_ref[:, i * xh:(i + 1) * xh])
            e = jnp.exp(s - jnp.max(s, axis=-1, keepdims=True))
            val["p", i] = (e * (1.0 / jnp.sum(e, axis=-1, keepdims=True))).astype(BF16)
        return run

    def att_v(i):
        def run():
            val["att", i] = _dot(val["p", i], kv_ref[:, d + i * xh:d + (i + 1) * xh]).astype(BF16)
        return run

    def att_out(i):
        def run():
            if i == 0:
                val["att_cat"] = jnp.concatenate([val["att", j] for j in range(X_HEADS)], axis=-1)
            val["xa", i] = _dot(val["att_cat"], wo_ref[:, i * mxu_n:(i + 1) * mxu_n])
        return run

    def ln2():
        xa = jnp.concatenate([val["xa", i] for i in range(d // mxu_n)], axis=-1)
        o_ref[...] = _layer_norm(alpha * val["h1"] + xa, l2g_ref[...], l2b_ref[...])

    att_start = n_half + 1
    att_streams = [(att_start + i, [att_q(i), att_p(i), att_v(i)]) for i in range(X_HEADS)]
    att_done = att_start + X_HEADS + 2
    out_stream = (att_done, [att_out(i) for i in range(d // mxu_n)] + [ln2])

    def gla_setup():
        z = _dot(val["a_low"], w_a2_ref[...]) + b_a_ref[...]
        g = _log_sigmoid(z) * (1.0 / GATE_TAU)
        val["g_hi"] = g.astype(BF16)
        val["g_lo"] = (g - val["g_hi"].astype(F32)).astype(BF16)
        val["v16"] = jnp.concatenate([val["v", i] for i in range(n_vslab)], axis=-1)

    row_i = lax.broadcasted_iota(jnp.int32, (ch, ch), 0)
    col_i = lax.broadcasted_iota(jnp.int32, (ch, ch), 1)
    causal = row_i >= col_i
    tri16 = causal.astype(BF16)
    lane_i = lax.broadcasted_iota(jnp.int32, (ch, slab), 1)
    own = [(lane_i >= hh * head_k) & (lane_i < (hh + 1) * head_k) for hh in range(2)]
    qm = {}
    o_intra = {}
    upd = {}
    decay = {}

    def gla_decay(ci):
        def run():
            rs = slice(ci * ch, (ci + 1) * ch)
            bcum = _dot(tri16, val["g_hi"][rs]) + _dot(tri16, val["g_lo"][rs])
            b_last = bcum[ch - 1:ch, :]
            qe = val["q"][rs] * jnp.exp(bcum)
            val["ke", ci] = (val["k"][rs] * jnp.exp(-bcum)).astype(BF16)
            val["kd", ci] = (val["k"][rs] * jnp.exp(b_last - bcum)).astype(BF16)
            decay[ci] = jnp.exp(b_last)
            for hd in range(GLA_HEADS):
                j, hh = divmod(hd, 2)
                sl = slice(slab * j, slab * (j + 1))
                qm[ci, hd] = jnp.where(own[hh], qe[:, sl], 0.0).astype(BF16)
        return run

    def gla_intra(ci):
        def run():
            rs = slice(ci * ch, (ci + 1) * ch)
            for hd in range(GLA_HEADS):
                j = hd // 2
                sl = slice(slab * j, slab * (j + 1))
                a = jnp.where(causal, _dot_nt(qm[ci, hd], val["ke", ci][:, sl]), 0.0)
                v_h = val["v16"][rs, hd * head_v:(hd + 1) * head_v]
                o_intra[ci, hd] = _dot(a.astype(BF16), v_h)
                upd[ci, hd] = _dot_tn(v_h, val["kd", ci][:, sl])
        return run

    state = {}

    def gla_inter(ci):
        def run():
            for hd in range(GLA_HEADS):
                j = hd // 2
                sl = slice(slab * j, slab * (j + 1))
                st = st_s[hd] if ci == 0 else state[hd]
                val["o", ci, hd] = o_intra[ci, hd] + _dot_nt(qm[ci, hd], st.astype(BF16))
                state[hd] = st * decay[ci][:, sl] + upd[ci, hd]
                if ci == n_ch - 1:
                    st_s[hd] = state[hd]
        return run

    def gla_norm(hd):
        def run():
            o = jnp.concatenate([val["o", ci, hd] for ci in range(n_ch)], axis=0)
            ms = jnp.mean(o * o, axis=-1, keepdims=True)
            r = val["r", hd // 2][:, (hd % 2) * head_v:(hd % 2 + 1) * head_v]
            val["gla", hd] = (o * lax.rsqrt(ms + LN_EPS) * gn_ref[...] * _silu(r)).astype(BF16)
        return run

    gla_streams = ([(gla_inputs_ready, [gla_setup])]
                   + [(gla_inputs_ready + 1 + ci, [gla_decay(ci), gla_intra(ci), gla_inter(ci)])
                      for ci in range(n_ch)]
                   + [(gla_inputs_ready + n_ch + 3, [gla_norm(hd) for hd in range(GLA_HEADS)])])

    norm_blocks = NORM_ROWS // CONV_ROWS

    def conv_norm(i):
        def run():
            y = jnp.concatenate(
                [jnp.concatenate([val["y", rb, half] for half in range(n_half)], axis=-1)
                 for rb in range(i * norm_blocks, (i + 1) * norm_blocks)], axis=0)
            y = _layer_norm(y + cb_ref[...], cg_ref[...], cbeta_ref[...])
            val["conv", i] = _silu(y).astype(BF16)
        return run

    n_norm = rows // CONV_ROWS // norm_blocks
    norm_streams = [(last_copy + (i + 1) * norm_blocks, [conv_norm(i)]) for i in range(n_norm)]

    _interleave(*vpu_streams, (0, mxu_pieces), (n_half, [ln1]), *att_streams, out_stream,
                *gla_streams, *norm_streams)
    u_s[0, 0:HALO, :] = u_s[0, rows:n_u, :]
    conv_out = jnp.concatenate([val["conv", i] for i in range(n_norm)], axis=0)
    xo_s[...] = jnp.concatenate([conv_out] + [val["gla", hd] for hd in range(GLA_HEADS)], axis=-1)


def _const_spec(shape, layer):
    nd = len(shape)
    return pl.BlockSpec((None,) + tuple(shape), lambda *_: (layer,) + (0,) * nd,
                        pipeline_mode=pl.Buffered(1))


def _mix_att(h2, kv, layer, p, alpha, seq_len):
    n, d = h2.shape
    rows = MIX_ROWS
    tiles = n // rows
    tiles_per_seq = seq_len // rows
    taps, _, c = p["conv_w"].shape[1:]
    dk = p["w_a2"].shape[2]
    dv = (p["w_main"].shape[2] - 2 * c - 2 * dk) // 2
    m = kv.shape[2]
    kern = functools.partial(_mix_att_kernel, alpha=alpha, tiles_per_seq=tiles_per_seq, taps=taps,
                             conv_dim=c, key_dim=dk, val_dim=dv)
    front = lambda s: (jnp.minimum(s, tiles - 1), 0)
    back = lambda s: (jnp.maximum(s - 1, 0), 0)
    names_a = ["w_main", "w_al", "w_a2", "b_a", "conv_w", "conv_b", "conv_ln_g", "conv_ln_b",
               "gla_norm_g", "w_mix_out", "ln1_g", "ln1_b", "w_xq"]
    names_b = ["w_xo", "ln2_g", "ln2_b"]
    kv_spec = pl.BlockSpec((None, None, m, 2 * d),
                           lambda s: (layer, jnp.maximum(s - 1, 0) // tiles_per_seq, 0, 0))
    return pl.pallas_call(
        kern,
        out_shape=jax.ShapeDtypeStruct((n, d), F32),
        grid=(tiles + 1,),
        in_specs=[pl.BlockSpec((rows, d), front), pl.BlockSpec((rows, d), back)]
        + [_const_spec(p[k].shape[1:], layer) for k in names_a] + [kv_spec]
        + [_const_spec(p[k].shape[1:], layer) for k in names_b],
        out_specs=pl.BlockSpec((rows, d), back),
        scratch_shapes=[
            pltpu.VMEM((SUBLANES, rows + HALO, c), F32),
            pltpu.VMEM((GLA_HEADS, dv // GLA_HEADS, 2 * dk // GLA_HEADS), F32),
            pltpu.VMEM((rows, c + dv), BF16),
        ],
        compiler_params=pltpu.CompilerParams(
            dimension_semantics=("arbitrary",),
            vmem_limit_bytes=V7X_VMEM_LIMIT_BYTES),
        name="mix_att",
    )(h2, h2, *[p[k] for k in names_a], kv, *[p[k] for k in names_b])


def _ffn_kernel(h_ref, w1_ref, w2_ref, lg_ref, lb_ref, o_ref, *, alpha, chunks):
    h = h_ref[...]
    h16 = h.astype(BF16)
    dff = w2_ref.shape[0]
    ff = None
    start = 0
    for width in chunks:
        g = _dot(h16, w1_ref[:, start:start + width])
        u = _dot(h16, w1_ref[:, dff + start:dff + start + width])
        part = _dot((_silu(g) * u).astype(BF16), w2_ref[start:start + width, :])
        ff = part if ff is None else ff + part
        start += width
    o_ref[...] = _layer_norm(alpha * h + ff, lg_ref[...], lb_ref[...])


def _ffn_chunks(dff, mxu_cols=2 * LANES, target=3):
    tiles = dff // mxu_cols
    assert tiles * mxu_cols == dff
    out = []
    while tiles > 0:
        take = min(target, tiles)
        out.append(take * mxu_cols)
        tiles -= take
    return tuple(out)


def _ffn(h2, layer, p, alpha):
    n, d = h2.shape
    rows = FFN_ROWS
    dff = p["w_ffn_out"].shape[1]
    names = ["w_ffn_in", "w_ffn_out", "ln3_g", "ln3_b"]
    return pl.pallas_call(
        functools.partial(_ffn_kernel, alpha=alpha, chunks=_ffn_chunks(dff)),
        out_shape=jax.ShapeDtypeStruct((n, d), F32),
        grid=(n // rows,),
        in_specs=[pl.BlockSpec((rows, d), lambda i: (i, 0))]
        + [_const_spec(p[k].shape[1:], layer) for k in names],
        out_specs=pl.BlockSpec((rows, d), lambda i: (i, 0)),
        compiler_params=pltpu.CompilerParams(
            dimension_semantics=("parallel",),
            vmem_limit_bytes=V7X_VMEM_LIMIT_BYTES),
        name="ffn",
    )(h2, *[p[k] for k in names])


def kernel(x, mem, ln0_g, ln0_b, w_in, w_a2, b_a, conv_w, conv_b, conv_ln_g, conv_ln_b, gla_norm_g,
           w_mix_out, ln1_g, ln1_b, w_xq, w_xkv, w_xo, ln2_g, ln2_b, w_ffn_in, w_ffn_out, ln3_g,
           ln3_b):
    bsz, t, d = x.shape
    depth = w_in.shape[0]
    rank = w_a2.shape[1]
    alpha = (2 * depth) ** 0.25
    n_main = w_in.shape[2] - rank

    row = lambda a: a.reshape(depth, 1, a.shape[-1])
    p = {
        "w_main": w_in[:, :, :n_main].astype(BF16),
        "w_al": jnp.pad(w_in[:, :, n_main:], ((0, 0), (0, 0), (0, LANES - rank))).astype(BF16),
        "w_a2": jnp.pad(w_a2, ((0, 0), (0, LANES - rank), (0, 0))).astype(BF16),
        "w_mix_out": w_mix_out.astype(BF16),
        "w_xq": w_xq.astype(BF16),
        "w_xo": w_xo.astype(BF16),
        "w_ffn_in": w_ffn_in.astype(BF16),
        "w_ffn_out": w_ffn_out.astype(BF16),
        "conv_w": jnp.broadcast_to(conv_w[:, :, None, :],
                                   conv_w.shape[:2] + (SUBLANES, conv_w.shape[2])),
        "b_a": row(b_a), "conv_b": row(conv_b), "conv_ln_g": row(conv_ln_g),
        "conv_ln_b": row(conv_ln_b), "gla_norm_g": row(gla_norm_g),
        "ln1_g": row(ln1_g), "ln1_b": row(ln1_b), "ln2_g": row(ln2_g), "ln2_b": row(ln2_b),
        "ln3_g": row(ln3_g), "ln3_b": row(ln3_b),
    }
    kv = _memory_kv(mem, w_xkv.astype(BF16))
    h = _entry_layer_norm(x.reshape(bsz * t, d), ln0_g, ln0_b)
    for layer in range(depth):
        h = _mix_att(h, kv, layer, p, alpha, t)
        h = _ffn(h, layer, p, alpha)
    return h.reshape(bsz, t, d)
```

```python
import functools

import jax
import jax.numpy as jnp
from jax import lax
from jax.experimental import pallas as pl
from jax.experimental.pallas import tpu as pltpu

F32 = jnp.float32
BF16 = jnp.bfloat16

LN_EPS = 1e-5
GATE_TAU = 16.0
GLA_CHUNK = 64
GLA_HEADS = 4
X_HEADS = 4

LANES = 128
SUBLANES = 8
V7X_VMEM_LIMIT_BYTES = 60000 * 1024

MIX_ROWS = 512
FFN_ROWS = 1024
LN_ROWS = 1024
CONV_ROWS = 32
NORM_ROWS = 128
HALO = 32

_NT = (((1,), (1,)), ((), ()))
_TN = (((0,), (0,)), ((), ()))


def _dot(a, b):
    return jnp.dot(a, b, preferred_element_type=F32)


def _dot_nt(a, b):
    return lax.dot_general(a, b, _NT, preferred_element_type=F32)


def _dot_tn(a, b):
    return lax.dot_general(a, b, _TN, preferred_element_type=F32)


def _layer_norm(x, g, b):
    mu = jnp.mean(x, axis=-1, keepdims=True)
    xc = x - mu
    var = jnp.mean(xc * xc, axis=-1, keepdims=True)
    return xc * lax.rsqrt(var + LN_EPS) * g + b


def _sigmoid(x):
    return 0.5 * jnp.tanh(0.5 * x) + 0.5


def _silu(x):
    half = 0.5 * x
    return half * jnp.tanh(half) + half


def _log_sigmoid(x):
    return jnp.minimum(x, 0.0) - jnp.log(1.0 + jnp.exp(-jnp.abs(x)))


def _ln_kernel(x_ref, g_ref, b_ref, o_ref):
    o_ref[...] = _layer_norm(x_ref[...], g_ref[...], b_ref[...])


def _entry_layer_norm(x2, g, b):
    n, d = x2.shape
    return pl.pallas_call(
        _ln_kernel,
        out_shape=jax.ShapeDtypeStruct((n, d), F32),
        grid=(n // LN_ROWS,),
        in_specs=[pl.BlockSpec((LN_ROWS, d), lambda i: (i, 0)),
                  pl.BlockSpec((1, d), lambda i: (0, 0)),
                  pl.BlockSpec((1, d), lambda i: (0, 0))],
        out_specs=pl.BlockSpec((LN_ROWS, d), lambda i: (i, 0)),
        compiler_params=pltpu.CompilerParams(dimension_semantics=("parallel",)),
        name="entry_ln",
    )(x2, g.reshape(1, d), b.reshape(1, d))


def _kv_kernel(mem_ref, w_ref, o_ref):
    o_ref[...] = _dot(mem_ref[...].astype(BF16), w_ref[...]).astype(BF16)


def _memory_kv(mem, w_xkv16):
    bsz, m, d = mem.shape
    depth, _, n = w_xkv16.shape
    return pl.pallas_call(
        _kv_kernel,
        out_shape=jax.ShapeDtypeStruct((depth, bsz, m, n), BF16),
        grid=(depth, bsz),
        in_specs=[pl.BlockSpec((None, m, d), lambda l, b: (b, 0, 0)),
                  pl.BlockSpec((None, d, n), lambda l, b: (l, 0, 0))],
        out_specs=pl.BlockSpec((None, None, m, n), lambda l, b: (l, b, 0, 0)),
        compiler_params=pltpu.CompilerParams(dimension_semantics=("arbitrary", "arbitrary")),
        name="memory_kv",
    )(mem, w_xkv16)


def _interleave(*streams):
    for rnd in range(max(first + len(s) for first, s in streams)):
        for first, s in streams:
            if 0 <= rnd - first < len(s):
                s[rnd - first]()


def _mix_att_kernel(hf_ref, hb_ref, w_in_ref, w_al_ref, w_a2_ref, b_a_ref, cw_ref, cb_ref, cg_ref,
                    cbeta_ref, gn_ref, w_out_ref, l1g_ref, l1b_ref, wq_ref, kv_ref, wo_ref, l2g_ref,
                    l2b_ref, o_ref, u_s, st_s, xo_s, *, alpha, tiles_per_seq, taps, conv_dim,
                    key_dim, val_dim):
    rows, d = hf_ref.shape
    c = conv_dim
    dk = key_dim
    dv = val_dim
    head_v = dv // GLA_HEADS
    head_k = dk // GLA_HEADS
    slab = 2 * head_k
    n_u = rows + HALO
    ch = GLA_CHUNK
    n_ch = rows // ch
    xh = d // X_HEADS
    mxu_n = 2 * LANES
    step = pl.program_id(0)

    @pl.when(step == 0)
    def _():
        xo_s[...] = jnp.zeros(xo_s.shape, xo_s.dtype)

    @pl.when(step % tiles_per_seq == 0)
    def _():
        u_s[0, 0:HALO, :] = jnp.zeros((HALO, c), F32)
        st_s[...] = jnp.zeros(st_s.shape, F32)

    val = {}

    h16 = hf_ref[...].astype(BF16)
    o_q = 2 * c
    o_v = o_q + 2 * dk
    o_r = o_v + dv
    n_vslab = dv // mxu_n

    first = HALO - (taps - 1)
    n_grp = n_u // SUBLANES
    n_half = c // mxu_n
    sub_i = lax.broadcasted_iota(jnp.int32, (n_grp - 1, SUBLANES, mxu_n), 1)
    grp_per_block = CONV_ROWS // SUBLANES

    def glu(half):
        def run():
            lanes = slice(half * mxu_n, (half + 1) * mxu_n)
            ca = _dot(h16, w_in_ref[:, half * mxu_n:(half + 1) * mxu_n])
            cgate = _dot(h16, w_in_ref[:, c + half * mxu_n:c + (half + 1) * mxu_n])
            u_s[0, HALO:n_u, lanes] = ca * _sigmoid(cgate)
        return run

    def shifted_copy(s, half):
        def run():
            lanes = slice(half * mxu_n, (half + 1) * mxu_n)
            x = u_s[0, :, lanes].reshape(n_grp, SUBLANES, mxu_n)
            mixed = jnp.where(sub_i >= s, x[:-1], x[1:])
            out = pltpu.roll(mixed, SUBLANES - s, 1)
            u_s[s, 0:n_u - SUBLANES, lanes] = out.reshape(n_u - SUBLANES, mxu_n)
        return run

    def conv_block(rb, half):
        def run():
            lanes = slice(half * mxu_n, (half + 1) * mxu_n)
            r0 = rb * CONV_ROWS
            acc = None
            for k in range(taps):
                a, s = divmod(first + k, SUBLANES)
                lo = r0 + SUBLANES * a
                x = u_s[s, lo:lo + CONV_ROWS, lanes].reshape(grp_per_block, SUBLANES, mxu_n)
                term = cw_ref[k, :, lanes] * x
                acc = term if acc is None else acc + term
            val["y", rb, half] = acc.reshape(CONV_ROWS, mxu_n)
        return run

    def proj(name, w_ref, start, width, post=lambda t: t):
        def run():
            val[name] = post(_dot(h16, w_ref[:, start:start + width]))
        return run

    glu(0)()
    vpu_streams = [(half, [shifted_copy(s, half) for s in range(1, SUBLANES)]
                    + [conv_block(rb, half) for rb in range(rows // CONV_ROWS)])
                   for half in range(n_half)]
    last_copy = n_half - 1 + SUBLANES - 1
    def out_proj():
        val["mix"] = _dot(xo_s[...], w_out_ref[...])

    mxu_pieces = ([glu(half) for half in range(1, n_half)] + [out_proj]
                  + [proj("q", w_in_ref, o_q, dk, lambda t: t * (head_k ** -0.5)),
                     proj("k", w_in_ref, o_q + dk, dk),
                     proj("a_low", w_al_ref, 0, LANES, lambda t: t.astype(BF16))]
                  + [proj(("v", i), w_in_ref, o_v + i * mxu_n, mxu_n, lambda t: t.astype(BF16))
                     for i in range(n_vslab)]
                  + [proj(("r", i), w_in_ref, o_r + i * mxu_n, mxu_n) for i in range(n_vslab)])
    gla_inputs_ready = n_half + 3 + n_vslab

    def ln1():
        val["h1"] = _layer_norm(alpha * hb_ref[...] + val["mix"], l1g_ref[...], l1b_ref[...])
        val["h1_16"] = val["h1"].astype(BF16)

    def att_q(i):
        def run():
            sl = slice(i * xh, (i + 1) * xh)
            val["qx", i] = (_dot(val["h1_16"], wq_ref[:, sl]) * (xh ** -0.5)).astype(BF16)
        return run

    def att_p(i):
        def run():
            s = _dot_nt(val["qx", i], kv_ref[:, i * xh:(i + 1) * xh])
            e = jnp.exp(s - jnp.max(s, axis=-1, keepdims=True))
            val["p", i] = (e * (1.0 / jnp.sum(e, axis=-1, keepdims=True))).astype(BF16)
        return run

    def att_v(i):
        def run():
            val["att", i] = _dot(val["p", i], kv_ref[:, d + i * xh:d + (i + 1) * xh]).astype(BF16)
        return run

    def att_out(i):
        def run():
            if i == 0:
                val["att_cat"] = jnp.concatenate([val["att", j] for j in range(X_HEADS)], axis=-1)
            val["xa", i] = _dot(val["att_cat"], wo_ref[:, i * mxu_n:(i + 1) * mxu_n])
        return run

    def ln2():
        xa = jnp.concatenate([val["xa", i] for i in range(d // mxu_n)], axis=-1)
        o_ref[...] = _layer_norm(alpha * val["h1"] + xa, l2g_ref[...], l2b_ref[...])

    att_start = n_half + 1
    att_streams = [(att_start + i, [att_q(i), att_p(i), att_v(i)]) for i in range(X_HEADS)]
    att_done = att_start + X_HEADS + 2
    out_stream = (att_done, [att_out(i) for i in range(d // mxu_n)] + [ln2])

    def gla_setup():
        z = _dot(val["a_low"], w_a2_ref[...]) + b_a_ref[...]
        g = _log_sigmoid(z) * (1.0 / GATE_TAU)
        val["g_hi"] = g.astype(BF16)
        val["g_lo"] = (g - val["g_hi"].astype(F32)).astype(BF16)
        val["v16"] = jnp.concatenate([val["v", i] for i in range(n_vslab)], axis=-1)

    row_i = lax.broadcasted_iota(jnp.int32, (ch, ch), 0)
    col_i = lax.broadcasted_iota(jnp.int32, (ch, ch), 1)
    causal = row_i >= col_i
    tri16 = causal.astype(BF16)
    lane_i = lax.broadcasted_iota(jnp.int32, (ch, slab), 1)
    own = [(lane_i >= hh * head_k) & (lane_i < (hh + 1) * head_k) for hh in range(2)]
    qm = {}
    o_intra = {}
    upd = {}
    decay = {}

    def gla_decay(ci):
        def run():
            rs = slice(ci * ch, (ci + 1) * ch)
            bcum = _dot(tri16, val["g_hi"][rs]) + _dot(tri16, val["g_lo"][rs])
            b_last = bcum[ch - 1:ch, :]
            qe = val["q"][rs] * jnp.exp(bcum)
            val["ke", ci] = (val["k"][rs] * jnp.exp(-bcum)).astype(BF16)
            val["kd", ci] = (val["k"][rs] * jnp.exp(b_last - bcum)).astype(BF16)
            decay[ci] = jnp.exp(b_last)
            for hd in range(GLA_HEADS):
                j, hh = divmod(hd, 2)
                sl = slice(slab * j, slab * (j + 1))
                qm[ci, hd] = jnp.where(own[hh], qe[:, sl], 0.0).astype(BF16)
        return run

    def gla_intra(ci):
        def run():
            rs = slice(ci * ch, (ci + 1) * ch)
            for hd in range(GLA_HEADS):
                j = hd // 2
                sl = slice(slab * j, slab * (j + 1))
                a = jnp.where(causal, _dot_nt(qm[ci, hd], val["ke", ci][:, sl]), 0.0)
                v_h = val["v16"][rs, hd * head_v:(hd + 1) * head_v]
                o_intra[ci, hd] = _dot(a.astype(BF16), v_h)
                upd[ci, hd] = _dot_tn(v_h, val["kd", ci][:, sl])
        return run

    state = {}

    def gla_inter(ci):
        def run():
            for hd in range(GLA_HEADS):
                j = hd // 2
                sl = slice(slab * j, slab * (j + 1))
                st = st_s[hd] if ci == 0 else state[hd]
                val["o", ci, hd] = o_intra[ci, hd] + _dot_nt(qm[ci, hd], st.astype(BF16))
                state[hd] = st * decay[ci][:, sl] + upd[ci, hd]
                if ci == n_ch - 1:
                    st_s[hd] = state[hd]
        return run

    def gla_norm(hd):
        def run():
            o = jnp.concatenate([val["o", ci, hd] for ci in range(n_ch)], axis=0)
            ms = jnp.mean(o * o, axis=-1, keepdims=True)
            r = val["r", hd // 2][:, (hd % 2) * head_v:(hd % 2 + 1) * head_v]
            val["gla", hd] = (o * lax.rsqrt(ms + LN_EPS) * gn_ref[...] * _silu(r)).astype(BF16)
        return run

    gla_streams = ([(gla_inputs_ready, [gla_setup])]
                   + [(gla_inputs_ready + 1 + ci // 2, [gla_decay(ci), gla_intra(ci), gla_inter(ci)])
                      for ci in range(n_ch)]
                   + [(gla_inputs_ready + n_ch + 3, [gla_norm(hd) for hd in range(GLA_HEADS)])])

    norm_blocks = NORM_ROWS // CONV_ROWS

    def conv_norm(i):
        def run():
            y = jnp.concatenate(
                [jnp.concatenate([val["y", rb, half] for half in range(n_half)], axis=-1)
                 for rb in range(i * norm_blocks, (i + 1) * norm_blocks)], axis=0)
            y = _layer_norm(y + cb_ref[...], cg_ref[...], cbeta_ref[...])
            val["conv", i] = _silu(y).astype(BF16)
        return run

    n_norm = rows // CONV_ROWS // norm_blocks
    norm_streams = [(last_copy + (i + 1) * norm_blocks, [conv_norm(i)]) for i in range(n_norm)]

    _interleave(*vpu_streams, (0, mxu_pieces), (n_half, [ln1]), *att_streams, out_stream,
                *gla_streams, *norm_streams)
    u_s[0, 0:HALO, :] = u_s[0, rows:n_u, :]
    conv_out = jnp.concatenate([val["conv", i] for i in range(n_norm)], axis=0)
    xo_s[...] = jnp.concatenate([conv_out] + [val["gla", hd] for hd in range(GLA_HEADS)], axis=-1)


def _const_spec(shape, layer):
    nd = len(shape)
    return pl.BlockSpec((None,) + tuple(shape), lambda *_: (layer,) + (0,) * nd,
                        pipeline_mode=pl.Buffered(1))


def _mix_att(h2, kv, layer, p, alpha, seq_len):
    n, d = h2.shape
    rows = MIX_ROWS
    tiles = n // rows
    tiles_per_seq = seq_len // rows
    taps, _, c = p["conv_w"].shape[1:]
    dk = p["w_a2"].shape[2]
    dv = (p["w_main"].shape[2] - 2 * c - 2 * dk) // 2
    m = kv.shape[2]
    kern = functools.partial(_mix_att_kernel, alpha=alpha, tiles_per_seq=tiles_per_seq, taps=taps,
                             conv_dim=c, key_dim=dk, val_dim=dv)
    front = lambda s: (jnp.minimum(s, tiles - 1), 0)
    back = lambda s: (jnp.maximum(s - 1, 0), 0)
    names_a = ["w_main", "w_al", "w_a2", "b_a", "conv_w", "conv_b", "conv_ln_g", "conv_ln_b",
               "gla_norm_g", "w_mix_out", "ln1_g", "ln1_b", "w_xq"]
    names_b = ["w_xo", "ln2_g", "ln2_b"]
    kv_spec = pl.BlockSpec((None, None, m, 2 * d),
                           lambda s: (layer, jnp.maximum(s - 1, 0) // tiles_per_seq, 0, 0))
    return pl.pallas_call(
        kern,
        out_shape=jax.ShapeDtypeStruct((n, d), F32),
        grid=(tiles + 1,),
        in_specs=[pl.BlockSpec((rows, d), front), pl.BlockSpec((rows, d), back)]
        + [_const_spec(p[k].shape[1:], layer) for k in names_a] + [kv_spec]
        + [_const_spec(p[k].shape[1:], layer) for k in names_b],
        out_specs=pl.BlockSpec((rows, d), back),
        scratch_shapes=[
            pltpu.VMEM((SUBLANES, rows + HALO, c), F32),
            pltpu.VMEM((GLA_HEADS, dv // GLA_HEADS, 2 * dk // GLA_HEADS), F32),
            pltpu.VMEM((rows, c + dv), BF16),
        ],
        compiler_params=pltpu.CompilerParams(
            dimension_semantics=("arbitrary",),
            vmem_limit_bytes=V7X_VMEM_LIMIT_BYTES),
        name="mix_att",
    )(h2, h2, *[p[k] for k in names_a], kv, *[p[k] for k in names_b])


def _ffn_kernel(h_ref, w1_ref, w2_ref, lg_ref, lb_ref, o_ref, *, alpha, chunks):
    h = h_ref[...]
    h16 = h.astype(BF16)
    dff = w2_ref.shape[0]
    ff = None
    start = 0
    for width in chunks:
        g = _dot(h16, w1_ref[:, start:start + width])
        u = _dot(h16, w1_ref[:, dff + start:dff + start + width])
        part = _dot((_silu(g) * u).astype(BF16), w2_ref[start:start + width, :])
        ff = part if ff is None else ff + part
        start += width
    o_ref[...] = _layer_norm(alpha * h + ff, lg_ref[...], lb_ref[...])


def _ffn_chunks(dff, mxu_cols=2 * LANES, target=3):
    tiles = dff // mxu_cols
    assert tiles * mxu_cols == dff
    out = []
    while tiles > 0:
        take = min(target, tiles)
        out.append(take * mxu_cols)
        tiles -= take
    return tuple(out)


def _ffn(h2, layer, p, alpha):
    n, d = h2.shape
    rows = FFN_ROWS
    dff = p["w_ffn_out"].shape[1]
    names = ["w_ffn_in", "w_ffn_out", "ln3_g", "ln3_b"]
    return pl.pallas_call(
        functools.partial(_ffn_kernel, alpha=alpha, chunks=_ffn_chunks(dff)),
        out_shape=jax.ShapeDtypeStruct((n, d), F32),
        grid=(n // rows,),
        in_specs=[pl.BlockSpec((rows, d), lambda i: (i, 0))]
        + [_const_spec(p[k].shape[1:], layer) for k in names],
        out_specs=pl.BlockSpec((rows, d), lambda i: (i, 0)),
        compiler_params=pltpu.CompilerParams(
            dimension_semantics=("parallel",),
            vmem_limit_bytes=V7X_VMEM_LIMIT_BYTES),
        name="ffn",
    )(h2, *[p[k] for k in names])


def kernel(x, mem, ln0_g, ln0_b, w_in, w_a2, b_a, conv_w, conv_b, conv_ln_g, conv_ln_b, gla_norm_g,
           w_mix_out, ln1_g, ln1_b, w_xq, w_xkv, w_xo, ln2_g, ln2_b, w_ffn_in, w_ffn_out, ln3_g,
           ln3_b):
    bsz, t, d = x.shape
    depth = w_in.shape[0]
    rank = w_a2.shape[1]
    alpha = (2 * depth) ** 0.25
    n_main = w_in.shape[2] - rank

    row = lambda a: a.reshape(depth, 1, a.shape[-1])
    p = {
        "w_main": w_in[:, :, :n_main].astype(BF16),
        "w_al": jnp.pad(w_in[:, :, n_main:], ((0, 0), (0, 0), (0, LANES - rank))).astype(BF16),
        "w_a2": jnp.pad(w_a2, ((0, 0), (0, LANES - rank), (0, 0))).astype(BF16),
        "w_mix_out": w_mix_out.astype(BF16),
        "w_xq": w_xq.astype(BF16),
        "w_xo": w_xo.astype(BF16),
        "w_ffn_in": w_ffn_in.astype(BF16),
        "w_ffn_out": w_ffn_out.astype(BF16),
        "conv_w": jnp.broadcast_to(conv_w[:, :, None, :],
                                   conv_w.shape[:2] + (SUBLANES, conv_w.shape[2])),
        "b_a": row(b_a), "conv_b": row(conv_b), "conv_ln_g": row(conv_ln_g),
        "conv_ln_b": row(conv_ln_b), "gla_norm_g": row(gla_norm_g),
        "ln1_g": row(ln1_g), "ln1_b": row(ln1_b), "ln2_g": row(ln2_g), "ln2_b": row(ln2_b),
        "ln3_g": row(ln3_g), "ln3_b": row(ln3_b),
    }
    kv = _memory_kv(mem, w_xkv.astype(BF16))
    h = _entry_layer_norm(x.reshape(bsz * t, d), ln0_g, ln0_b)
    for layer in range(depth):
        h = _mix_att(h, kv, layer, p, alpha, t)
        h = _ffn(h, layer, p, alpha)
    return h.reshape(bsz, t, d)
```

```python
import functools

import jax
import jax.numpy as jnp
from jax import lax
from jax.experimental import pallas as pl
from jax.experimental.pallas import tpu as pltpu

F32 = jnp.float32
BF16 = jnp.bfloat16

LN_EPS = 1e-5
GATE_TAU = 16.0
GLA_CHUNK = 64
GLA_HEADS = 4
X_HEADS = 4

LANES = 128
SUBLANES = 8
V7X_VMEM_LIMIT_BYTES = 60000 * 1024

MIX_ROWS = 512
FFN_ROWS = 1024
LN_ROWS = 1024
CONV_ROWS = 32
NORM_ROWS = 128
HALO = 32

_NT = (((1,), (1,)), ((), ()))
_TN = (((0,), (0,)), ((), ()))


def _dot(a, b):
    return jnp.dot(a, b, preferred_element_type=F32)


def _dot_nt(a, b):
    return lax.dot_general(a, b, _NT, preferred_element_type=F32)


def _dot_tn(a, b):
    return lax.dot_general(a, b, _TN, preferred_element_type=F32)


def _layer_norm(x, g, b):
    mu = jnp.mean(x, axis=-1, keepdims=True)
    xc = x - mu
    var = jnp.mean(xc * xc, axis=-1, keepdims=True)
    return xc * lax.rsqrt(var + LN_EPS) * g + b


def _sigmoid(x):
    return 0.5 * jnp.tanh(0.5 * x) + 0.5


def _silu(x):
    half = 0.5 * x
    return half * jnp.tanh(half) + half


def _log_sigmoid(x):
    return jnp.minimum(x, 0.0) - jnp.log(1.0 + jnp.exp(-jnp.abs(x)))


def _ln_kernel(x_ref, g_ref, b_ref, o_ref):
    o_ref[...] = _layer_norm(x_ref[...], g_ref[...], b_ref[...])


def _entry_layer_norm(x2, g, b):
    n, d = x2.shape
    return pl.pallas_call(
        _ln_kernel,
        out_shape=jax.ShapeDtypeStruct((n, d), F32),
        grid=(n // LN_ROWS,),
        in_specs=[pl.BlockSpec((LN_ROWS, d), lambda i: (i, 0)),
                  pl.BlockSpec((1, d), lambda i: (0, 0)),
                  pl.BlockSpec((1, d), lambda i: (0, 0))],
        out_specs=pl.BlockSpec((LN_ROWS, d), lambda i: (i, 0)),
        compiler_params=pltpu.CompilerParams(dimension_semantics=("parallel",)),
        name="entry_ln",
    )(x2, g.reshape(1, d), b.reshape(1, d))


def _kv_kernel(mem_ref, w_ref, o_ref):
    o_ref[...] = _dot(mem_ref[...].astype(BF16), w_ref[...]).astype(BF16)


def _memory_kv(mem, w_xkv16):
    bsz, m, d = mem.shape
    depth, _, n = w_xkv16.shape
    return pl.pallas_call(
        _kv_kernel,
        out_shape=jax.ShapeDtypeStruct((depth, bsz, m, n), BF16),
        grid=(depth, bsz),
        in_specs=[pl.BlockSpec((None, m, d), lambda l, b: (b, 0, 0)),
                  pl.BlockSpec((None, d, n), lambda l, b: (l, 0, 0))],
        out_specs=pl.BlockSpec((None, None, m, n), lambda l, b: (l, b, 0, 0)),
        compiler_params=pltpu.CompilerParams(dimension_semantics=("arbitrary", "arbitrary")),
        name="memory_kv",
    )(mem, w_xkv16)


def _interleave(*streams):
    for rnd in range(max(first + len(s) for first, s in streams)):
        for first, s in streams:
            if 0 <= rnd - first < len(s):
                s[rnd - first]()


def _mix_att_kernel(hf_ref, hb_ref, w_in_ref, w_a2_ref, b_a_ref, cw_ref, cb_ref, cg_ref,
                    cbeta_ref, gn_ref, w_out_ref, l1g_ref, l1b_ref, wq_ref, kv_ref, wo_ref, l2g_ref,
                    l2b_ref, o_ref, u_s, st_s, xo_s, *, alpha, tiles_per_seq, taps, conv_dim,
                    key_dim, val_dim):
    rows, d = hf_ref.shape
    c = conv_dim
    dk = key_dim
    dv = val_dim
    head_v = dv // GLA_HEADS
    head_k = dk // GLA_HEADS
    slab = 2 * head_k
    n_u = rows + HALO
    ch = GLA_CHUNK
    n_ch = rows // ch
    xh = d // X_HEADS
    mxu_n = 2 * LANES
    step = pl.program_id(0)

    @pl.when(step == 0)
    def _():
        xo_s[...] = jnp.zeros(xo_s.shape, xo_s.dtype)

    @pl.when(step % tiles_per_seq == 0)
    def _():
        u_s[0, 0:HALO, :] = jnp.zeros((HALO, c), F32)
        st_s[...] = jnp.zeros(st_s.shape, F32)

    val = {}

    h16 = hf_ref[...].astype(BF16)
    o_q = 2 * c
    o_v = o_q + 2 * dk
    o_r = o_v + dv
    n_vslab = dv // mxu_n

    first = HALO - (taps - 1)
    n_grp = n_u // SUBLANES
    n_half = c // mxu_n
    sub_i = lax.broadcasted_iota(jnp.int32, (n_grp - 1, SUBLANES, mxu_n), 1)
    grp_per_block = CONV_ROWS // SUBLANES

    def glu(half):
        def run():
            lanes = slice(half * mxu_n, (half + 1) * mxu_n)
            ca = _dot(h16, w_in_ref[:, half * mxu_n:(half + 1) * mxu_n])
            cgate = _dot(h16, w_in_ref[:, c + half * mxu_n:c + (half + 1) * mxu_n])
            u_s[0, HALO:n_u, lanes] = ca * _sigmoid(cgate)
        return run

    def shifted_copy(s, half):
        def run():
            lanes = slice(half * mxu_n, (half + 1) * mxu_n)
            x = u_s[0, :, lanes].reshape(n_grp, SUBLANES, mxu_n)
            mixed = jnp.where(sub_i >= s, x[:-1], x[1:])
            out = pltpu.roll(mixed, SUBLANES - s, 1)
            u_s[s, 0:n_u - SUBLANES, lanes] = out.reshape(n_u - SUBLANES, mxu_n)
        return run

    def conv_block(rb, half):
        def run():
            lanes = slice(half * mxu_n, (half + 1) * mxu_n)
            r0 = rb * CONV_ROWS
            acc = None
            for k in range(taps):
                a, s = divmod(first + k, SUBLANES)
                lo = r0 + SUBLANES * a
                x = u_s[s, lo:lo + CONV_ROWS, lanes].reshape(grp_per_block, SUBLANES, mxu_n)
                term = cw_ref[k, :, lanes] * x
                acc = term if acc is None else acc + term
            val["y", rb, half] = acc.reshape(CONV_ROWS, mxu_n)
        return run

    def proj(name, w_ref, start, width, post=lambda t: t):
        def run():
            val[name] = post(_dot(h16, w_ref[:, start:start + width]))
        return run

    glu(0)()
    vpu_streams = [(half, [shifted_copy(s, half) for s in range(1, SUBLANES)]
                    + [conv_block(rb, half) for rb in range(rows // CONV_ROWS)])
                   for half in range(n_half)]
    last_copy = n_half - 1 + SUBLANES - 1
    def out_proj():
        val["mix"] = _dot(xo_s[...], w_out_ref[...])

    mxu_pieces = ([glu(half) for half in range(1, n_half)] + [out_proj]
                  + [proj("q", w_in_ref, o_q, dk, lambda t: t * (head_k ** -0.5)),
                     proj("k", w_in_ref, o_q + dk, dk),
                     proj("a_low", w_in_ref, o_r + dv, LANES, lambda t: t.astype(BF16))]
                  + [proj(("v", i), w_in_ref, o_v + i * mxu_n, mxu_n, lambda t: t.astype(BF16))
                     for i in range(n_vslab)]
                  + [proj(("r", i), w_in_ref, o_r + i * mxu_n, mxu_n) for i in range(n_vslab)])
    gla_inputs_ready = n_half + 3 + n_vslab

    def ln1():
        val["h1"] = _layer_norm(alpha * hb_ref[...] + val["mix"], l1g_ref[...], l1b_ref[...])
        val["h1_16"] = val["h1"].astype(BF16)

    def att_q(i):
        def run():
            sl = slice(i * xh, (i + 1) * xh)
            val["qx", i] = (_dot(val["h1_16"], wq_ref[:, sl]) * (xh ** -0.5)).astype(BF16)
        return run

    def att_p(i):
        def run():
            s = _dot_nt(val["qx", i], kv_ref[:, i * xh:(i + 1) * xh])
            e = jnp.exp(s - jnp.max(s, axis=-1, keepdims=True))
            val["p", i] = (e * (1.0 / jnp.sum(e, axis=-1, keepdims=True))).astype(BF16)
        return run

    def att_v(i):
        def run():
            val["att", i] = _dot(val["p", i], kv_ref[:, d + i * xh:d + (i + 1) * xh]).astype(BF16)
        return run

    def att_out(i):
        def run():
            if i == 0:
                val["att_cat"] = jnp.concatenate([val["att", j] for j in range(X_HEADS)], axis=-1)
            val["xa", i] = _dot(val["att_cat"], wo_ref[:, i * mxu_n:(i + 1) * mxu_n])
        return run

    def ln2():
        xa = jnp.concatenate([val["xa", i] for i in range(d // mxu_n)], axis=-1)
        o_ref[...] = _layer_norm(alpha * val["h1"] + xa, l2g_ref[...], l2b_ref[...])

    att_start = n_half + 1
    att_streams = [(att_start + i, [att_q(i), att_p(i), att_v(i)]) for i in range(X_HEADS)]
    att_done = att_start + X_HEADS + 2
    out_stream = (att_done, [att_out(i) for i in range(d // mxu_n)] + [ln2])

    def gla_setup():
        z = _dot(val["a_low"], w_a2_ref[...]) + b_a_ref[...]
        g = _log_sigmoid(z) * (1.0 / GATE_TAU)
        val["g_hi"] = g.astype(BF16)
        val["g_lo"] = (g - val["g_hi"].astype(F32)).astype(BF16)
        val["v16"] = jnp.concatenate([val["v", i] for i in range(n_vslab)], axis=-1)

    row_i = lax.broadcasted_iota(jnp.int32, (ch, ch), 0)
    col_i = lax.broadcasted_iota(jnp.int32, (ch, ch), 1)
    causal = row_i >= col_i
    tri16 = causal.astype(BF16)
    lane_i = lax.broadcasted_iota(jnp.int32, (ch, slab), 1)
    own = [(lane_i >= hh * head_k) & (lane_i < (hh + 1) * head_k) for hh in range(2)]
    qm = {}
    o_intra = {}
    upd = {}
    decay = {}

    def gla_decay(ci):
        def run():
            rs = slice(ci * ch, (ci + 1) * ch)
            bcum = _dot(tri16, val["g_hi"][rs]) + _dot(tri16, val["g_lo"][rs])
            b_last = bcum[ch - 1:ch, :]
            qe = val["q"][rs] * jnp.exp(bcum)
            val["ke", ci] = (val["k"][rs] * jnp.exp(-bcum)).astype(BF16)
            val["kd", ci] = (val["k"][rs] * jnp.exp(b_last - bcum)).astype(BF16)
            decay[ci] = jnp.exp(b_last)
            for hd in range(GLA_HEADS):
                j, hh = divmod(hd, 2)
                sl = slice(slab * j, slab * (j + 1))
                qm[ci, hd] = jnp.where(own[hh], qe[:, sl], 0.0).astype(BF16)
        return run

    def gla_intra(ci):
        def run():
            rs = slice(ci * ch, (ci + 1) * ch)
            for hd in range(GLA_HEADS):
                j = hd // 2
                sl = slice(slab * j, slab * (j + 1))
                a = jnp.where(causal, _dot_nt(qm[ci, hd], val["ke", ci][:, sl]), 0.0)
                v_h = val["v16"][rs, hd * head_v:(hd + 1) * head_v]
                o_intra[ci, hd] = _dot(a.astype(BF16), v_h)
                upd[ci, hd] = _dot_tn(v_h, val["kd", ci][:, sl])
        return run

    state = {}

    def gla_inter(ci):
        def run():
            for hd in range(GLA_HEADS):
                j = hd // 2
                sl = slice(slab * j, slab * (j + 1))
                st = st_s[hd] if ci == 0 else state[hd]
                val["o", ci, hd] = o_intra[ci, hd] + _dot_nt(qm[ci, hd], st.astype(BF16))
                state[hd] = st * decay[ci][:, sl] + upd[ci, hd]
                if ci == n_ch - 1:
                    st_s[hd] = state[hd]
        return run

    def gla_norm(hd):
        def run():
            o = jnp.concatenate([val["o", ci, hd] for ci in range(n_ch)], axis=0)
            ms = jnp.mean(o * o, axis=-1, keepdims=True)
            r = val["r", hd // 2][:, (hd % 2) * head_v:(hd % 2 + 1) * head_v]
            val["gla", hd] = (o * lax.rsqrt(ms + LN_EPS) * gn_ref[...] * _silu(r)).astype(BF16)
        return run

    gla_streams = ([(gla_inputs_ready, [gla_setup])]
                   + [(gla_inputs_ready + 1 + ci // 2, [gla_decay(ci), gla_intra(ci), gla_inter(ci)])
                      for ci in range(n_ch)]
                   + [(gla_inputs_ready + n_ch + 3, [gla_norm(hd) for hd in range(GLA_HEADS)])])

    norm_blocks = NORM_ROWS // CONV_ROWS

    def conv_norm(i):
        def run():
            y = jnp.concatenate(
                [jnp.concatenate([val["y", rb, half] for half in range(n_half)], axis=-1)
                 for rb in range(i * norm_blocks, (i + 1) * norm_blocks)], axis=0)
            y = _layer_norm(y + cb_ref[...], cg_ref[...], cbeta_ref[...])
            val["conv", i] = _silu(y).astype(BF16)
        return run

    n_norm = rows // CONV_ROWS // norm_blocks
    norm_streams = [(last_copy + (i + 1) * norm_blocks, [conv_norm(i)]) for i in range(n_norm)]

    _interleave(*vpu_streams, (0, mxu_pieces), (n_half, [ln1]), *att_streams, out_stream,
                *gla_streams, *norm_streams)
    u_s[0, 0:HALO, :] = u_s[0, rows:n_u, :]
    conv_out = jnp.concatenate([val["conv", i] for i in range(n_norm)], axis=0)
    xo_s[...] = jnp.concatenate([conv_out] + [val["gla", hd] for hd in range(GLA_HEADS)], axis=-1)


def _const_spec(shape, layer):
    nd = len(shape)
    return pl.BlockSpec((None,) + tuple(shape), lambda *_: (layer,) + (0,) * nd,
                        pipeline_mode=pl.Buffered(1))


def _mix_att(h2, kv, layer, p, alpha, seq_len):
    n, d = h2.shape
    rows = MIX_ROWS
    tiles = n // rows
    tiles_per_seq = seq_len // rows
    taps, _, c = p["conv_w"].shape[1:]
    dk = p["w_a2"].shape[2]
    dv = (p["w_main"].shape[2] - LANES - 2 * c - 2 * dk) // 2
    m = kv.shape[2]
    kern = functools.partial(_mix_att_kernel, alpha=alpha, tiles_per_seq=tiles_per_seq, taps=taps,
                             conv_dim=c, key_dim=dk, val_dim=dv)
    front = lambda s: (jnp.minimum(s, tiles - 1), 0)
    back = lambda s: (jnp.maximum(s - 1, 0), 0)
    names_a = ["w_main", "w_a2", "b_a", "conv_w", "conv_b", "conv_ln_g", "conv_ln_b",
               "gla_norm_g", "w_mix_out", "ln1_g", "ln1_b", "w_xq"]
    names_b = ["w_xo", "ln2_g", "ln2_b"]
    kv_spec = pl.BlockSpec((None, None, m, 2 * d),
                           lambda s: (layer, jnp.maximum(s - 1, 0) // tiles_per_seq, 0, 0))
    return pl.pallas_call(
        kern,
        out_shape=jax.ShapeDtypeStruct((n, d), F32),
        grid=(tiles + 1,),
        in_specs=[pl.BlockSpec((rows, d), front), pl.BlockSpec((rows, d), back)]
        + [_const_spec(p[k].shape[1:], layer) for k in names_a] + [kv_spec]
        + [_const_spec(p[k].shape[1:], layer) for k in names_b],
        out_specs=pl.BlockSpec((rows, d), back),
        scratch_shapes=[
            pltpu.VMEM((SUBLANES, rows + HALO, c), F32),
            pltpu.VMEM((GLA_HEADS, dv // GLA_HEADS, 2 * dk // GLA_HEADS), F32),
            pltpu.VMEM((rows, c + dv), BF16),
        ],
        compiler_params=pltpu.CompilerParams(
            dimension_semantics=("arbitrary",),
            vmem_limit_bytes=V7X_VMEM_LIMIT_BYTES),
        name="mix_att",
    )(h2, h2, *[p[k] for k in names_a], kv, *[p[k] for k in names_b])


def _ffn_kernel(h_ref, w1_ref, w2_ref, lg_ref, lb_ref, o_ref, *, alpha, chunks):
    h = h_ref[...]
    h16 = h.astype(BF16)
    dff = w2_ref.shape[0]
    ff = None
    start = 0
    for width in chunks:
        g = _dot(h16, w1_ref[:, start:start + width])
        u = _dot(h16, w1_ref[:, dff + start:dff + start + width])
        part = _dot((_silu(g) * u).astype(BF16), w2_ref[start:start + width, :])
        ff = part if ff is None else ff + part
        start += width
    o_ref[...] = _layer_norm(alpha * h + ff, lg_ref[...], lb_ref[...])


def _ffn_chunks(dff, mxu_cols=2 * LANES, target=3):
    tiles = dff // mxu_cols
    assert tiles * mxu_cols == dff
    out = []
    while tiles > 0:
        take = min(target, tiles)
        out.append(take * mxu_cols)
        tiles -= take
    return tuple(out)


def _ffn(h2, layer, p, alpha):
    n, d = h2.shape
    rows = FFN_ROWS
    dff = p["w_ffn_out"].shape[1]
    names = ["w_ffn_in", "w_ffn_out", "ln3_g", "ln3_b"]
    return pl.pallas_call(
        functools.partial(_ffn_kernel, alpha=alpha, chunks=_ffn_chunks(dff)),
        out_shape=jax.ShapeDtypeStruct((n, d), F32),
        grid=(n // rows,),
        in_specs=[pl.BlockSpec((rows, d), lambda i: (i, 0))]
        + [_const_spec(p[k].shape[1:], layer) for k in names],
        out_specs=pl.BlockSpec((rows, d), lambda i: (i, 0)),
        compiler_params=pltpu.CompilerParams(
            dimension_semantics=("parallel",),
            vmem_limit_bytes=V7X_VMEM_LIMIT_BYTES),
        name="ffn",
    )(h2, *[p[k] for k in names])


def kernel(x, mem, ln0_g, ln0_b, w_in, w_a2, b_a, conv_w, conv_b, conv_ln_g, conv_ln_b, gla_norm_g,
           w_mix_out, ln1_g, ln1_b, w_xq, w_xkv, w_xo, ln2_g, ln2_b, w_ffn_in, w_ffn_out, ln3_g,
           ln3_b):
    bsz, t, d = x.shape
    depth = w_in.shape[0]
    rank = w_a2.shape[1]
    alpha = (2 * depth) ** 0.25

    row = lambda a: a.reshape(depth, 1, a.shape[-1])
    p = {
        "w_main": jnp.pad(w_in, ((0, 0), (0, 0), (0, LANES - rank))).astype(BF16),
        "w_a2": jnp.pad(w_a2, ((0, 0), (0, LANES - rank), (0, 0))).astype(BF16),
        "w_mix_out": w_mix_out.astype(BF16),
        "w_xq": w_xq.astype(BF16),
        "w_xo": w_xo.astype(BF16),
        "w_ffn_in": w_ffn_in.astype(BF16),
        "w_ffn_out": w_ffn_out.astype(BF16),
        "conv_w": jnp.broadcast_to(conv_w[:, :, None, :],
                                   conv_w.shape[:2] + (SUBLANES, conv_w.shape[2])),
        "b_a": row(b_a), "conv_b": row(conv_b), "conv_ln_g": row(conv_ln_g),
        "conv_ln_b": row(conv_ln_b), "gla_norm_g": row(gla_norm_g),
        "ln1_g": row(ln1_g), "ln1_b": row(ln1_b), "ln2_g": row(ln2_g), "ln2_b": row(ln2_b),
        "ln3_g": row(ln3_g), "ln3_b": row(ln3_b),
    }
    kv = _memory_kv(mem, w_xkv.astype(BF16))
    h = _entry_layer_norm(x.reshape(bsz * t, d), ln0_g, ln0_b)
    for layer in range(depth):
        h = _mix_att(h, kv, layer, p, alpha, t)
        h = _ffn(h, layer, p, alpha)
    return h.reshape(bsz, t, d)
```
